```python
import jax, jax.numpy as jnp
from jax import lax
import numpy as np

D_MODEL = 1024
BATCH = 16
SEQ = 2048
DEPTH = 4

GRID_W = 64
CTX_LEN = 256
NA_HEADS = 6
NA_HEAD_DIM = 64
NA_WIDTH = NA_HEADS * NA_HEAD_DIM
NA_WIN_H = 8
NA_WIN_W = 16
MLA_HEADS = 6
MLA_Q_RANK = 256
MLA_KV_RANK = 128
MLA_NOPE = 64
MLA_ROPE = 32
MLA_V = 64
LRU_WIDTH = 256
LRU_HEADS = 4
LRU_BLOCK = LRU_WIDTH // LRU_HEADS
LRU_CONV_W = 4
LRU_C = 8.0
MIX_WIDTH = NA_WIDTH + MLA_HEADS * MLA_V + LRU_WIDTH
FFN_HIDDEN = -(-(8 * D_MODEL) // (3 * 256)) * 256
IN_SPLITS = (NA_WIDTH, NA_WIDTH, NA_WIDTH, MLA_Q_RANK, MLA_KV_RANK, MLA_ROPE, LRU_WIDTH, LRU_WIDTH)
Q_BLOCK = 128
ROPE_BASE = 10000.0
EPS = 1e-6
NA_SCALE = NA_HEAD_DIM ** -0.5
MLA_SCALE = (MLA_NOPE + MLA_ROPE) ** -0.5

kernel_name = 'hybrid_na_mla_rglru_dit_block'


def _rmsnorm(x, g):
    xf = x.astype(jnp.float32)
    y = xf * lax.rsqrt(jnp.mean(xf * xf, axis=-1, keepdims=True) + EPS)
    return (y * g.astype(jnp.float32)).astype(x.dtype)


def _heads(t, n):
    return t.reshape(t.shape[:-1] + (n, t.shape[-1] // n))


def _adaln(cvec, w, b):
    m = jax.nn.silu(cvec) @ w + b
    return jnp.split(m[:, None, :], 6, axis=-1)


def _modulate(x, shift, scale):
    return x * (1.0 + scale) + shift


def _split_cols(u):
    return jnp.split(u, np.cumsum(IN_SPLITS)[:-1].tolist(), axis=-1)


def _axial_angles(n_tok, rot_dim):
    t = jnp.arange(n_tok)
    row = (t // GRID_W).astype(jnp.float32)
    col = (t % GRID_W).astype(jnp.float32)
    n_freq = rot_dim // 4
    inv = ROPE_BASE ** (-jnp.arange(n_freq, dtype=jnp.float32) / n_freq)
    return row[:, None] * inv, col[:, None] * inv


def _rope_half(z, ang):
    z1, z2 = jnp.split(z, 2, axis=-1)
    cos = jnp.cos(ang)[:, None, :]
    sin = jnp.sin(ang)[:, None, :]
    return jnp.concatenate([z1 * cos - z2 * sin, z1 * sin + z2 * cos], axis=-1)


def _axial_rope(z, angs):
    zr, zc = jnp.split(z, 2, axis=-1)
    return jnp.concatenate([_rope_half(zr, angs[0]), _rope_half(zc, angs[1])], axis=-1).astype(z.dtype)


def _attend(q, k, v, scale):
    s = jnp.einsum('bqhd,bkhd->bhqk', q, k, preferred_element_type=jnp.float32) * scale
    p = jax.nn.softmax(s, axis=-1).astype(v.dtype)
    return jnp.einsum('bhqk,bkhd->bqhd', p, v)


def _neighborhood_attention(q, k, v, k_ctx, v_ctx, rpb, rows):
    B, S, H, dh = q.shape
    kh = min(NA_WIN_H, rows)
    ncb = GRID_W // NA_WIN_W
    span = 2 * NA_WIN_W
    qcol = np.arange(GRID_W).reshape(ncb, NA_WIN_W)
    kcol = np.clip(qcol[:, :1] - NA_WIN_W // 2, 0, GRID_W - span) + np.arange(span)
    qcs = np.clip(qcol - NA_WIN_W // 2, 0, GRID_W - NA_WIN_W)[..., None]
    col_ok = (kcol[:, None, :] >= qcs) & (kcol[:, None, :] < qcs + NA_WIN_W)
    col_rel = np.clip(kcol[:, None, :] - qcol[..., None], 1 - NA_WIN_W, NA_WIN_W - 1) + NA_WIN_W - 1
    mask = np.broadcast_to(col_ok[:, :, None, :], (ncb, NA_WIN_W, kh, span)).reshape(ncb, NA_WIN_W, kh * span)
    qg = q.reshape(B, rows, GRID_W, H, dh)
    kg = k.reshape(B, rows, GRID_W, H, dh)
    vg = v.reshape(B, rows, GRID_W, H, dh)
    n_lat = kh * span

    def row_block(r):
        rs = jnp.clip(r - kh // 2, 0, rows - kh)
        k_rows = lax.dynamic_slice_in_dim(kg, rs, kh, axis=1)
        v_rows = lax.dynamic_slice_in_dim(vg, rs, kh, axis=1)
        k_blk = k_rows[:, :, kcol].transpose(0, 2, 1, 3, 4, 5).reshape(B, ncb, n_lat, H, dh)
        v_blk = v_rows[:, :, kcol].transpose(0, 2, 1, 3, 4, 5).reshape(B, ncb, n_lat, H, dh)
        q_blk = lax.dynamic_index_in_dim(qg, r, axis=1, keepdims=False).reshape(B, ncb, NA_WIN_W, H, dh)
        row_rel = rs + jnp.arange(kh) - r + NA_WIN_H - 1
        bias = rpb[:, row_rel][:, :, col_rel]
        bias = bias.transpose(0, 2, 3, 1, 4).reshape(H, ncb, NA_WIN_W, n_lat).astype(jnp.float32)
        s_lat = jnp.einsum('bnqhd,bnkhd->bhnqk', q_blk, k_blk, preferred_element_type=jnp.float32) * NA_SCALE + bias
        s_lat = jnp.where(mask, s_lat, -jnp.inf)
        s_ctx = jnp.einsum('bnqhd,bchd->bhnqc', q_blk, k_ctx, preferred_element_type=jnp.float32) * NA_SCALE
        p = jax.nn.softmax(jnp.concatenate([s_lat, s_ctx], axis=-1), axis=-1).astype(v.dtype)
        o = (jnp.einsum('bhnqk,bnkhd->bnqhd', p[..., :n_lat], v_blk)
             + jnp.einsum('bhnqc,bchd->bnqhd', p[..., n_lat:], v_ctx))
        return o.reshape(B, GRID_W, H, dh)

    out = lax.map(row_block, jnp.arange(rows))
    return out.transpose(1, 0, 2, 3, 4).reshape(B, S, H, dh)


def _mla_q(cq, cq_g, w_qb, q_g, angs):
    q = _heads(_rmsnorm(cq, cq_g) @ w_qb, MLA_HEADS)
    q_nope = _rmsnorm(q[..., :MLA_NOPE], q_g[:MLA_NOPE])
    q_rope = _rmsnorm(q[..., MLA_NOPE:], q_g[MLA_NOPE:])
    if angs is not None:
        q_rope = _axial_rope(q_rope, angs)
    return jnp.concatenate([q_nope, q_rope], axis=-1)


def _mla_kv(ckv, kr, ckv_g, w_kvb, k_g, angs):
    kv = _heads(_rmsnorm(ckv, ckv_g) @ w_kvb, MLA_HEADS)
    k_nope = _rmsnorm(kv[..., :MLA_NOPE], k_g[:MLA_NOPE])
    v = kv[..., MLA_NOPE:]
    k_rope = _rmsnorm(kr, k_g[MLA_NOPE:])[..., None, :]
    if angs is not None:
        k_rope = _axial_rope(k_rope, angs)
    k_rope = jnp.broadcast_to(k_rope, k_nope.shape[:-1] + (MLA_ROPE,))
    return jnp.concatenate([k_nope, k_rope], axis=-1), v


def _mla_latent_attention(q, k, v, k_ctx, v_ctx):
    B, S, H, dq = q.shape
    kk = jnp.concatenate([k_ctx, k], axis=1)
    vv = jnp.concatenate([v_ctx, v], axis=1)
    qb = q.reshape(B, S // Q_BLOCK, Q_BLOCK, H, dq).swapaxes(0, 1)
    out = lax.map(lambda qi: _attend(qi, kk, vv, MLA_SCALE), qb)
    return out.swapaxes(0, 1).reshape(B, S, H, v.shape[-1])


def _dwconv(x, w, b):
    y = lax.conv_general_dilated(x, w[:, None, :].astype(x.dtype), window_strides=(1,),
                                 padding=[((LRU_CONV_W - 1) // 2, LRU_CONV_W // 2)],
                                 dimension_numbers=('NWC', 'WIO', 'NWC'), feature_group_count=x.shape[-1])
    return y + b


def _rglru_coeffs(xc, w_a, b_a, w_x, b_x, lam):
    B, L, W = xc.shape
    xf = xc.astype(jnp.float32)
    xh = xf.reshape(B, L, LRU_HEADS, LRU_BLOCK)
    gate_a = jnp.einsum('blhi,dhij->dblhj', xh, w_a.astype(jnp.float32)).reshape(2, B, L, W) + b_a.astype(jnp.float32)[:, None, None, :]
    gate_x = jnp.einsum('blhi,dhij->dblhj', xh, w_x.astype(jnp.float32)).reshape(2, B, L, W) + b_x.astype(jnp.float32)[:, None, None, :]
    log_a = -LRU_C * jax.nn.sigmoid(gate_a) * jax.nn.softplus(-lam.astype(jnp.float32))[:, None, None, :]
    a = jnp.exp(log_a)
    u = jnp.sqrt(-jnp.expm1(2.0 * log_a)) * (jax.nn.sigmoid(gate_x) * xf[None])
    return a, u


def _linear_scan(a, u, h0, reverse):
    def step(h, au):
        h = au[0] * h + au[1]
        return h, h
    h_last, hs = lax.scan(step, h0, (a.swapaxes(0, 1), u.swapaxes(0, 1)), reverse=reverse)
    return hs.swapaxes(0, 1), h_last


def _swiglu(h, w_gate, w_up, w_down):
    return (jax.nn.silu(h @ w_gate) * (h @ w_up)) @ w_down


def setup_inputs(seed: int = 0) -> dict:
    key = jax.random.key(seed)
    ks = iter(jax.random.split(key, 32))
    f32 = jnp.float32

    def nrm(shape, scale):
        return jax.random.normal(next(ks), shape, f32) * scale

    def gain(shape):
        return 1.0 + nrm(shape, 0.05)

    lam_u = jax.random.uniform(next(ks), (DEPTH, 2, LRU_WIDTH), f32, 0.9, 0.999)
    a0 = lam_u ** (1.0 / LRU_C)
    return {
        'x': nrm((BATCH, SEQ, D_MODEL), 1.0),
        'c': nrm((BATCH, D_MODEL), 1.0),
        'ctx': nrm((BATCH, CTX_LEN, D_MODEL), 1.0),
        'c_ctx': nrm((D_MODEL,), 1.0),
        'ada_w': nrm((DEPTH, D_MODEL, 6 * D_MODEL), 0.5 * D_MODEL ** -0.5),
        'ada_b': nrm((DEPTH, 6 * D_MODEL), 0.02),
        'norm_mix_g': gain((DEPTH, D_MODEL)),
        'norm_ffn_g': gain((DEPTH, D_MODEL)),
        'w_in': nrm((DEPTH, D_MODEL, sum(IN_SPLITS)), D_MODEL ** -0.5),
        'na_q_g': gain((DEPTH, NA_HEAD_DIM)),
        'na_k_g': gain((DEPTH, NA_HEAD_DIM)),
        'na_rpb': nrm((DEPTH, NA_HEADS, 2 * NA_WIN_H - 1, 2 * NA_WIN_W - 1), 0.2),
        'mla_cq_g': gain((DEPTH, MLA_Q_RANK)),
        'mla_w_qb': nrm((DEPTH, MLA_Q_RANK, MLA_HEADS * (MLA_NOPE + MLA_ROPE)), MLA_Q_RANK ** -0.5),
        'mla_ckv_g': gain((DEPTH, MLA_KV_RANK)),
        'mla_w_kvb': nrm((DEPTH, MLA_KV_RANK, MLA_HEADS * (MLA_NOPE + MLA_V)), MLA_KV_RANK ** -0.5),
        'mla_q_g': gain((DEPTH, MLA_NOPE + MLA_ROPE)),
        'mla_k_g': gain((DEPTH, MLA_NOPE + MLA_ROPE)),
        'lru_conv_w': nrm((DEPTH, LRU_CONV_W, LRU_WIDTH), LRU_CONV_W ** -0.5),
        'lru_conv_b': nrm((DEPTH, LRU_WIDTH), 0.02),
        'lru_w_a': nrm((DEPTH, 2, LRU_HEADS, LRU_BLOCK, LRU_BLOCK), LRU_BLOCK ** -0.5),
        'lru_b_a': nrm((DEPTH, 2, LRU_WIDTH), 0.02),
        'lru_w_x': nrm((DEPTH, 2, LRU_HEADS, LRU_BLOCK, LRU_BLOCK), LRU_BLOCK ** -0.5),
        'lru_b_x': nrm((DEPTH, 2, LRU_WIDTH), 0.02),
        'lru_lambda': jnp.log(a0) - jnp.log1p(-a0),
        'w_out': nrm((DEPTH, MIX_WIDTH, D_MODEL), MIX_WIDTH ** -0.5),
        'ffn_w_gate': nrm((DEPTH, D_MODEL, FFN_HIDDEN), D_MODEL ** -0.5),
        'ffn_w_up': nrm((DEPTH, D_MODEL, FFN_HIDDEN), D_MODEL ** -0.5),
        'ffn_w_down': nrm((DEPTH, FFN_HIDDEN, D_MODEL), FFN_HIDDEN ** -0.5),
    }


def reference(x, c, ctx, c_ctx, ada_w, ada_b, norm_mix_g, norm_ffn_g, w_in, na_q_g, na_k_g, na_rpb,
              mla_cq_g, mla_w_qb, mla_ckv_g, mla_w_kvb, mla_q_g, mla_k_g,
              lru_conv_w, lru_conv_b, lru_w_a, lru_b_a, lru_w_x, lru_b_x, lru_lambda,
              w_out, ffn_w_gate, ffn_w_up, ffn_w_down):
    B, S, _ = x.shape
    C = ctx.shape[1]
    rows = S // GRID_W
    angs = _axial_angles(S, MLA_ROPE)
    h, hc = x, ctx
    for l in range(DEPTH):
        last = l == DEPTH - 1
        sh_m, sc_m, g_m, sh_f, sc_f, g_f = _adaln(c, ada_w[l], ada_b[l])
        csh_m, csc_m, cg_m, csh_f, csc_f, cg_f = _adaln(c_ctx[None], ada_w[l], ada_b[l])

        u = _modulate(_rmsnorm(h, norm_mix_g[l]), sh_m, sc_m) @ w_in[l]
        uc = _modulate(_rmsnorm(hc, norm_mix_g[l]), csh_m, csc_m) @ w_in[l]
        na_q, na_k, na_v, cq, ckv, kr, lx, lg = _split_cols(u)
        cna_q, cna_k, cna_v, ccq, cckv, ckr, clx, clg = _split_cols(uc)

        q = _rmsnorm(_heads(na_q, NA_HEADS), na_q_g[l])
        k = _rmsnorm(_heads(na_k, NA_HEADS), na_k_g[l])
        v = _heads(na_v, NA_HEADS)
        kc = _rmsnorm(_heads(cna_k, NA_HEADS), na_k_g[l])
        vc = _heads(cna_v, NA_HEADS)
        na_o = _neighborhood_attention(q, k, v, kc, vc, na_rpb[l], rows)

        mq = _mla_q(cq, mla_cq_g[l], mla_w_qb[l], mla_q_g[l], angs)
        mk, mv = _mla_kv(ckv, kr, mla_ckv_g[l], mla_w_kvb[l], mla_k_g[l], angs)
        mkc, mvc = _mla_kv(cckv, ckr, mla_ckv_g[l], mla_w_kvb[l], mla_k_g[l], None)
        mla_o = _mla_latent_attention(mq, mk, mv, mkc, mvc)

        xcc = _dwconv(clx, lru_conv_w[l], lru_conv_b[l])
        a_c, u_c = _rglru_coeffs(xcc, lru_w_a[l], lru_b_a[l], lru_w_x[l], lru_b_x[l], lru_lambda[l])
        zeros = jnp.zeros((B, LRU_WIDTH), jnp.float32)
        hcf, hcf_last = _linear_scan(a_c[0], u_c[0], zeros, False)
        hcb, hcb_last = _linear_scan(a_c[1], u_c[1], zeros, True)
        xcl = _dwconv(lx, lru_conv_w[l], lru_conv_b[l])
        a_l, u_l = _rglru_coeffs(xcl, lru_w_a[l], lru_b_a[l], lru_w_x[l], lru_b_x[l], lru_lambda[l])
        hf, _ = _linear_scan(a_l[0], u_l[0], hcf_last, False)
        hb, _ = _linear_scan(a_l[1], u_l[1], hcb_last, True)
        lru_o = jax.nn.gelu(lg) * (hf + hb).astype(lg.dtype)

        o_lat = jnp.concatenate([na_o.reshape(B, S, NA_WIDTH), mla_o.reshape(B, S, MLA_HEADS * MLA_V), lru_o], axis=-1)
        h = h + g_m * (o_lat @ w_out[l])
        h = h + g_f * _swiglu(_modulate(_rmsnorm(h, norm_ffn_g[l]), sh_f, sc_f), ffn_w_gate[l], ffn_w_up[l], ffn_w_down[l])

        if not last:
            qc = _rmsnorm(_heads(cna_q, NA_HEADS), na_q_g[l])
            na_c = _attend(qc, kc, vc, NA_SCALE).reshape(B, C, NA_WIDTH)
            mqc = _mla_q(ccq, mla_cq_g[l], mla_w_qb[l], mla_q_g[l], None)
            mla_c = _attend(mqc, mkc, mvc, MLA_SCALE).reshape(B, C, MLA_HEADS * MLA_V)
            lru_c = jax.nn.gelu(clg) * (hcf + hcb).astype(clg.dtype)
            o_ctx = jnp.concatenate([na_c, mla_c, lru_c], axis=-1)
            hc = hc + cg_m * (o_ctx @ w_out[l])
            hc = hc + cg_f * _swiglu(_modulate(_rmsnorm(hc, norm_ffn_g[l]), csh_f, csc_f), ffn_w_gate[l], ffn_w_up[l], ffn_w_down[l])
    return h
```

```python
import functools

import numpy as np
import jax
import jax.numpy as jnp
from jax import lax
from jax.experimental import pallas as pl
from jax.experimental.pallas import tpu as pltpu

GRID_W = 64
NA_HEADS = 6
NA_HEAD_DIM = 64
NA_WIDTH = NA_HEADS * NA_HEAD_DIM
NA_WIN_H = 8
NA_WIN_W = 16
MLA_HEADS = 6
MLA_Q_RANK = 256
MLA_KV_RANK = 128
MLA_NOPE = 64
MLA_ROPE = 32
MLA_V = 64
LRU_WIDTH = 256
LRU_HEADS = 4
LRU_BLOCK = LRU_WIDTH // LRU_HEADS
LRU_CONV_W = 4
LRU_C = 8.0
ROPE_BASE = 10000.0
EPS = 1e-6
NA_SCALE = NA_HEAD_DIM ** -0.5
MLA_SCALE = (MLA_NOPE + MLA_ROPE) ** -0.5

LANE = 128
SUBLANE = 8
MLA_QK_W = MLA_HEADS * LANE
NEG_BIG = -1e30
VMEM_LIMIT = 56 * 1024 * 1024

C_Q, C_K, C_V = 0, NA_WIDTH, 2 * NA_WIDTH
C_CQ = 3 * NA_WIDTH
C_CKV = C_CQ + MLA_Q_RANK
C_KR = C_CKV + MLA_KV_RANK
C_KRP = C_KR + LANE
C_LX = C_KRP + LANE
C_LG = C_LX + LRU_WIDTH
W_ALL = C_LG + LRU_WIDTH

BF = jnp.bfloat16
F32 = jnp.float32


def _dot(a, b):
    return jnp.dot(a, b, preferred_element_type=F32)


def _dot_nt(a, b):
    return lax.dot_general(a, b, (((1,), (1,)), ((), ())), preferred_element_type=F32)


def _sigmoid(x):
    return 1.0 / (1.0 + jnp.exp(-x))


def _silu(x):
    return x * _sigmoid(x)


def _gelu_tanh(x):
    return 0.5 * x * (1.0 + jnp.tanh(0.7978845608028654 * (x + 0.044715 * (x * x * x))))


def _params(*sem):
    return pltpu.CompilerParams(dimension_semantics=sem, vmem_limit_bytes=VMEM_LIMIT)


def _const_spec(shape):
    nd = len(shape)
    return pl.BlockSpec(shape, lambda *_: (0,) * nd, pipeline_mode=pl.Buffered(1))


def _ada_kernel(c_ref, w_ref, b_ref, o_ref):
    s = _silu(c_ref[...]).astype(BF)
    o_ref[...] = _dot(s, w_ref[...].astype(BF)) + b_ref[...]


def _ada_call(cvec, ada_w, ada_b):
    depth, d, d6 = ada_w.shape
    r = cvec.shape[0]
    tn = 1024
    return pl.pallas_call(
        _ada_kernel,
        grid=(depth, d6 // tn),
        in_specs=[
            pl.BlockSpec((r, d), lambda l, j: (0, 0)),
            pl.BlockSpec((None, d, tn), lambda l, j: (l, 0, j)),
            pl.BlockSpec((None, 1, tn), lambda l, j: (l, 0, j)),
        ],
        out_specs=pl.BlockSpec((None, r, tn), lambda l, j: (l, 0, j)),
        out_shape=jax.ShapeDtypeStruct((depth, r, d6), F32),
        compiler_params=_params("parallel", "parallel"),
        name="adaln",
    )(cvec, ada_w, ada_b.reshape(depth, 1, d6))


def _in_kernel(h_ref, sh_ref, sc_ref, g_ref, wall_ref, wqb_ref, wkvb_ref, gna_ref, gcq_ref, gckv_ref,
               gm_ref, ena_ref, emla_ref, cos_ref, sin_ref,
               q_ref, k_ref, v_ref, mq_ref, mk_ref, mv_ref, lx_ref, lg_ref):
    x = h_ref[...]
    y = x * lax.rsqrt(jnp.mean(x * x, axis=-1, keepdims=True) + EPS) * g_ref[...]
    y = y * (1.0 + sc_ref[0]) + sh_ref[0]
    u = _dot(y.astype(BF), wall_ref[...])

    ena = ena_ref[...]
    for p in range(NA_WIDTH // LANE):
        lo = p * LANE
        uq = u[:, C_Q + lo:C_Q + lo + LANE]
        msq = _dot((uq * uq).astype(BF), ena)
        q_ref[:, lo:lo + LANE] = (uq * lax.rsqrt(msq + EPS) * gna_ref[0:1, lo:lo + LANE]).astype(BF)
        uk = u[:, C_K + lo:C_K + lo + LANE]
        msk = _dot((uk * uk).astype(BF), ena)
        k_ref[:, lo:lo + LANE] = (uk * lax.rsqrt(msk + EPS) * gna_ref[1:2, lo:lo + LANE]).astype(BF)
    v_ref[...] = u[:, C_V:C_V + NA_WIDTH].astype(BF)

    emla = emla_ref[...]
    cos = cos_ref[...]
    sin = sin_ref[...]
    cq = u[:, C_CQ:C_CQ + MLA_Q_RANK]
    cqn = cq * lax.rsqrt(jnp.mean(cq * cq, axis=-1, keepdims=True) + EPS) * gcq_ref[...]
    qq = _dot(cqn.astype(BF), wqb_ref[...])
    a_q = gm_ref[0:1, :] * cos
    b_q = gm_ref[1:2, :] * sin
    for hd in range(MLA_HEADS):
        lo = hd * LANE
        qr = qq[:, lo:lo + LANE]
        qp = qq[:, MLA_QK_W + lo:MLA_QK_W + lo + LANE]
        ms = _dot((qr * qr).astype(BF), emla)
        mq_ref[:, lo:lo + LANE] = (lax.rsqrt(ms + EPS) * (qr * a_q + qp * b_q)).astype(BF)

    ckv = u[:, C_CKV:C_CKV + MLA_KV_RANK]
    ckvn = ckv * lax.rsqrt(jnp.mean(ckv * ckv, axis=-1, keepdims=True) + EPS) * gckv_ref[...]
    kv = _dot(ckvn.astype(BF), wkvb_ref[...])
    krr = u[:, C_KR:C_KR + LANE]
    krp = u[:, C_KRP:C_KRP + LANE]
    msr = _dot((krr * krr).astype(BF), emla)
    kr = lax.rsqrt(msr + EPS) * (krr * (gm_ref[3:4, :] * cos) + krp * (gm_ref[4:5, :] * sin))
    for hd in range(MLA_HEADS):
        lo = hd * LANE
        kn = kv[:, lo:lo + LANE]
        ms = _dot((kn * kn).astype(BF), emla)
        mk_ref[:, lo:lo + LANE] = (kn * lax.rsqrt(ms + EPS) * gm_ref[2:3, :] + kr).astype(BF)
    mv_ref[...] = kv[:, MLA_QK_W:].astype(BF)

    lx_ref[...] = u[:, C_LX:C_LX + LRU_WIDTH]
    lg_ref[...] = u[:, C_LG:C_LG + LRU_WIDTH]


def _in_call(h, sh, sc, wts, cos, sin, tm, tiles_per_mod):
    n, d = h.shape
    nt = n // tm
    tab_tiles = cos.shape[0] // tm
    if tiles_per_mod is None:
        mod_map = lambda i: (0, 0, 0)
    else:
        mod_map = lambda i: (i // tiles_per_mod, 0, 0)
    row = lambda w: pl.BlockSpec((tm, w), lambda i: (i, 0))
    tab = pl.BlockSpec((tm, LANE), lambda i: (i % tab_tiles, 0))
    consts = [wts["g_mix"], wts["w_all"], wts["w_qb"], wts["w_kvb"], wts["g_na"], wts["g_cq"], wts["g_ckv"],
              wts["g_mla"], wts["e_na"], wts["e_mla"]]
    out_w = [(NA_WIDTH, BF), (NA_WIDTH, BF), (NA_WIDTH, BF), (MLA_QK_W, BF), (MLA_QK_W, BF),
             (MLA_HEADS * MLA_V, BF), (LRU_WIDTH, F32), (LRU_WIDTH, F32)]
    return pl.pallas_call(
        _in_kernel,
        grid=(nt,),
        in_specs=[row(d), pl.BlockSpec((1, 1, d), mod_map), pl.BlockSpec((1, 1, d), mod_map)]
                 + [_const_spec(a.shape) for a in consts] + [tab, tab],
        out_specs=[row(w) for w, _ in out_w],
        out_shape=[jax.ShapeDtypeStruct((n, w), dt) for w, dt in out_w],
        compiler_params=_params("parallel"),
        name="in_proj",
    )(h, sh, sc, *consts, cos, sin)


def _softmax_pv(s_a, s_b, v_a, v_b):
    m = jnp.maximum(jnp.max(s_a, axis=-1, keepdims=True), jnp.max(s_b, axis=-1, keepdims=True))
    p_a = jnp.exp(s_a - m)
    p_b = jnp.exp(s_b - m)
    l = jnp.sum(p_a, axis=-1, keepdims=True) + jnp.sum(p_b, axis=-1, keepdims=True)
    o = _dot(p_a.astype(BF), v_a) + _dot(p_b.astype(BF), v_b)
    return o / l


def _na_kernel(q_ref, k_ref, v_ref, kc_ref, vc_ref, bias_ref, o_ref, *, rows):
    r = pl.program_id(1)
    rs = jnp.clip(r - NA_WIN_H // 2, 0, rows - NA_WIN_H)
    start = pl.multiple_of(rs * GRID_W, GRID_W)
    nwin = NA_WIN_H * GRID_W
    first_half = lax.broadcasted_iota(jnp.int32, (GRID_W, LANE), 1) < NA_HEAD_DIM
    for p in range(NA_WIDTH // LANE):
        lo = p * LANE
        q2 = q_ref[:, lo:lo + LANE]
        k2 = k_ref[pl.ds(start, nwin), lo:lo + LANE]
        v2 = v_ref[pl.ds(start, nwin), lo:lo + LANE]
        kc2 = kc_ref[:, lo:lo + LANE]
        vc2 = vc_ref[:, lo:lo + LANE]
        outs = []
        for hh in range(2):
            qm = jnp.where(first_half if hh == 0 else ~first_half, q2, jnp.zeros_like(q2))
            s_w = _dot_nt(qm, k2) + bias_ref[2 * p + hh]
            s_c = _dot_nt(qm, kc2)
            outs.append(_softmax_pv(s_w, s_c, v2, vc2))
        o_ref[:, lo:lo + LANE] = jnp.where(first_half, outs[0], outs[1]).astype(BF)


def _na_call(q, k, v, kc, vc, bias, batch):
    n, w = q.shape
    s = n // batch
    c = kc.shape[0] // batch
    rows = s // GRID_W
    q3, k3, v3 = (a.reshape(batch, s, w) for a in (q, k, v))
    kc3, vc3 = (a.reshape(batch, c, w) for a in (kc, vc))
    nwin = NA_WIN_H * GRID_W

    def bias_map(b, r):
        return (r - jnp.clip(r - NA_WIN_H // 2, 0, rows - NA_WIN_H), 0, 0, 0)

    out = pl.pallas_call(
        functools.partial(_na_kernel, rows=rows),
        grid=(batch, rows),
        in_specs=[
            pl.BlockSpec((None, GRID_W, w), lambda b, r: (b, r, 0)),
            pl.BlockSpec((None, s, w), lambda b, r: (b, 0, 0)),
            pl.BlockSpec((None, s, w), lambda b, r: (b, 0, 0)),
            pl.BlockSpec((None, c, w), lambda b, r: (b, 0, 0)),
            pl.BlockSpec((None, c, w), lambda b, r: (b, 0, 0)),
            pl.BlockSpec((None, NA_HEADS, GRID_W, nwin), bias_map),
        ],
        out_specs=pl.BlockSpec((None, GRID_W, w), lambda b, r: (b, r, 0)),
        out_shape=jax.ShapeDtypeStruct((batch, s, w), BF),
        compiler_params=_params("parallel", "arbitrary"),
        name="nbr_attn",
    )(q3, k3, v3, kc3, vc3, bias)
    return out.reshape(n, w)


def _na_bias_table(rpb, rows):
    qc = np.arange(GRID_W)
    kcol = np.arange(GRID_W)
    cs = np.clip(qc - NA_WIN_W // 2, 0, GRID_W - NA_WIN_W)
    ok = (kcol[None, :] >= cs[:, None]) & (kcol[None, :] < cs[:, None] + NA_WIN_W)
    col_rel = np.clip(kcol[None, :] - qc[:, None], 1 - NA_WIN_W, NA_WIN_W - 1) + NA_WIN_W - 1
    d = np.arange(NA_WIN_H)
    row_rel = np.arange(NA_WIN_H)[None, :] - d[:, None] + NA_WIN_H - 1
    t = rpb[:, row_rel][:, :, :, col_rel]
    t = jnp.where(ok[None, None, None], t.astype(F32), NEG_BIG)
    t = t.transpose(1, 0, 3, 2, 4)
    return t.reshape(NA_WIN_H, NA_HEADS, GRID_W, NA_WIN_H * GRID_W)


def _mla_kernel(q_ref, k_ref, v_ref, kc_ref, vc_ref, o_ref):
    tq = q_ref.shape[0]
    first_half = lax.broadcasted_iota(jnp.int32, (tq, LANE), 1) < MLA_V
    for p in range(MLA_HEADS // 2):
        lo = p * LANE
        v2 = v_ref[:, lo:lo + LANE]
        vc2 = vc_ref[:, lo:lo + LANE]
        outs = []
        for hh in range(2):
            hl = (2 * p + hh) * LANE
            qh = q_ref[:, hl:hl + LANE]
            s_l = _dot_nt(qh, k_ref[:, hl:hl + LANE])
            s_c = _dot_nt(qh, kc_ref[:, hl:hl + LANE])
            outs.append(_softmax_pv(s_l, s_c, v2, vc2))
        o_ref[:, lo:lo + LANE] = jnp.where(first_half, outs[0], outs[1]).astype(BF)


def _mla_call(q, k, v, kc, vc, batch, tq):
    n, wq = q.shape
    wv = v.shape[1]
    s = n // batch
    c = kc.shape[0] // batch
    q3, k3 = q.reshape(batch, s, wq), k.reshape(batch, s, wq)
    v3 = v.reshape(batch, s, wv)
    kc3, vc3 = kc.reshape(batch, c, wq), vc.reshape(batch, c, wv)
    out = pl.pallas_call(
        _mla_kernel,
        grid=(batch, s // tq),
        in_specs=[
            pl.BlockSpec((None, tq, wq), lambda b, i: (b, i, 0)),
            pl.BlockSpec((None, s, wq), lambda b, i: (b, 0, 0)),
            pl.BlockSpec((None, s, wv), lambda b, i: (b, 0, 0)),
            pl.BlockSpec((None, c, wq), lambda b, i: (b, 0, 0)),
            pl.BlockSpec((None, c, wv), lambda b, i: (b, 0, 0)),
        ],
        out_specs=pl.BlockSpec((None, tq, wv), lambda b, i: (b, i, 0)),
        out_shape=jax.ShapeDtypeStruct((batch, s, wv), BF),
        compiler_params=_params("parallel", "arbitrary"),
        name="latent_attn",
    )(q3, k3, v3, kc3, vc3)
    return out.reshape(n, wv)


def _softmax_pv1(s, v):
    m = jnp.max(s, axis=-1, keepdims=True)
    p = jnp.exp(s - m)
    l = jnp.sum(p, axis=-1, keepdims=True)
    return _dot(p.astype(BF), v) / l


def _ctx_attn_kernel(q_ref, k_ref, v_ref, mq_ref, mk_ref, mv_ref, ona_ref, omla_ref):
    c = q_ref.shape[0]
    first_half = lax.broadcasted_iota(jnp.int32, (c, LANE), 1) < NA_HEAD_DIM
    for p in range(NA_WIDTH // LANE):
        lo = p * LANE
        q2 = q_ref[:, lo:lo + LANE]
        k2 = k_ref[:, lo:lo + LANE]
        v2 = v_ref[:, lo:lo + LANE]
        outs = []
        for hh in range(2):
            qm = jnp.where(first_half if hh == 0 else ~first_half, q2, jnp.zeros_like(q2))
            outs.append(_softmax_pv1(_dot_nt(qm, k2), v2))
        ona_ref[:, lo:lo + LANE] = jnp.where(first_half, outs[0], outs[1]).astype(BF)
    for p in range(MLA_HEADS // 2):
        lo = p * LANE
        v2 = mv_ref[:, lo:lo + LANE]
        outs = []
        for hh in range(2):
            hl = (2 * p + hh) * LANE
            outs.append(_softmax_pv1(_dot_nt(mq_ref[:, hl:hl + LANE], mk_ref[:, hl:hl + LANE]), v2))
        omla_ref[:, lo:lo + LANE] = jnp.where(first_half, outs[0], outs[1]).astype(BF)


def _ctx_attn_call(q, k, v, mq, mk, mv, batch):
    nc = q.shape[0]
    c = nc // batch
    ins = [q, k, v, mq, mk, mv]
    ins3 = [a.reshape(batch, c, a.shape[1]) for a in ins]
    spec = lambda w: pl.BlockSpec((None, c, w), lambda b: (b, 0, 0))
    ona, omla = pl.pallas_call(
        _ctx_attn_kernel,
        grid=(batch,),
        in_specs=[spec(a.shape[1]) for a in ins],
        out_specs=[spec(NA_WIDTH), spec(MLA_HEADS * MLA_V)],
        out_shape=[jax.ShapeDtypeStruct((batch, c, NA_WIDTH), BF),
                   jax.ShapeDtypeStruct((batch, c, MLA_HEADS * MLA_V), BF)],
        compiler_params=_params("parallel"),
        name="ctx_attn",
    )(*ins3)
    return ona.reshape(nc, NA_WIDTH), omla.reshape(nc, MLA_HEADS * MLA_V)


LRU_PAD = SUBLANE
LRU_CHUNK = 512


def _lru_coeffs(x_ref, n, xpad_ref, a_ref, u_ref, cw_ref, cb_ref, wg_ref, bg_ref, sp):
    w = LRU_WIDTH
    zeros = jnp.zeros((LRU_PAD, w), F32)
    xpad_ref[0:LRU_PAD, :] = zeros
    xpad_ref[LRU_PAD:LRU_PAD + n, :] = x_ref[...]
    xpad_ref[LRU_PAD + n:2 * LRU_PAD + n, :] = zeros
    left = (LRU_CONV_W - 1) // 2
    for c0 in range(0, n, LRU_CHUNK):
        cn = min(LRU_CHUNK, n - c0)
        xc = jnp.zeros((cn, w), F32) + cb_ref[...]
        for j in range(LRU_CONV_W):
            off = LRU_PAD + c0 + j - left
            xc = xc + cw_ref[j:j + 1, :] * xpad_ref[off:off + cn, :]
        g = _dot(xc.astype(BF), wg_ref[...]) + bg_ref[...]
        for dr in range(2):
            log_a = (-LRU_C) * _sigmoid(g[:, dr * w:(dr + 1) * w]) * sp[:, dr * w:(dr + 1) * w]
            a = jnp.exp(log_a)
            gx = _sigmoid(g[:, (2 + dr) * w:(3 + dr) * w])
            a_ref[dr, c0:c0 + cn, :] = a
            u_ref[dr, c0:c0 + cn, :] = jnp.sqrt(1.0 - a * a) * (gx * xc)


def _lru_scan(n, a_ref, u_ref, hf0, hb0):
    w = LRU_WIDTH
    nchunk = n // SUBLANE
    rowi = lax.broadcasted_iota(jnp.int32, (SUBLANE, w), 0)

    def body(c, carry):
        hf, hb = carry
        rf = pl.multiple_of(c * SUBLANE, SUBLANE)
        rb = pl.multiple_of((nchunk - 1 - c) * SUBLANE, SUBLANE)
        af = a_ref[0, pl.ds(rf, SUBLANE), :]
        uf = u_ref[0, pl.ds(rf, SUBLANE), :]
        ab = a_ref[1, pl.ds(rb, SUBLANE), :]
        ub = u_ref[1, pl.ds(rb, SUBLANE), :]
        for s in (1, 2, 4):
            mf = rowi >= s
            uf = jnp.where(mf, af * pltpu.roll(uf, s, 0) + uf, uf)
            af = jnp.where(mf, af * pltpu.roll(af, s, 0), af)
            mb = rowi < SUBLANE - s
            ub = jnp.where(mb, ab * pltpu.roll(ub, SUBLANE - s, 0) + ub, ub)
            ab = jnp.where(mb, ab * pltpu.roll(ab, SUBLANE - s, 0), ab)
        hfc = af * hf + uf
        hbc = ab * hb + ub
        u_ref[0, pl.ds(rf, SUBLANE), :] = hfc
        u_ref[1, pl.ds(rb, SUBLANE), :] = hbc
        return hfc[SUBLANE - 1:SUBLANE, :], hbc[0:1, :]

    return lax.fori_loop(0, nchunk, body, (hf0, hb0))


def _lru_kernel(lx_ref, lg_ref, clx_ref, clg_ref, cw_ref, cb_ref, wg_ref, bg_ref, lam_ref,
                o_ref, oc_ref, xpad_ref, a_ref, u_ref, ac_ref, uc_ref):
    n = lx_ref.shape[0]
    nc = clx_ref.shape[0]
    nlam = -lam_ref[...]
    sp = jnp.maximum(nlam, 0.0) + jnp.log(1.0 + jnp.exp(-jnp.abs(nlam)))
    zero = jnp.zeros((1, LRU_WIDTH), F32)
    _lru_coeffs(clx_ref, nc, xpad_ref, ac_ref, uc_ref, cw_ref, cb_ref, wg_ref, bg_ref, sp)
    hf0, hb0 = _lru_scan(nc, ac_ref, uc_ref, zero, zero)
    oc_ref[...] = (_gelu_tanh(clg_ref[...]) * (uc_ref[0] + uc_ref[1])).astype(BF)
    _lru_coeffs(lx_ref, n, xpad_ref, a_ref, u_ref, cw_ref, cb_ref, wg_ref, bg_ref, sp)
    _lru_scan(n, a_ref, u_ref, hf0, hb0)
    for c0 in range(0, n, LRU_CHUNK):
        cn = min(LRU_CHUNK, n - c0)
        hsum = u_ref[0, c0:c0 + cn, :] + u_ref[1, c0:c0 + cn, :]
        o_ref[c0:c0 + cn, :] = (_gelu_tanh(lg_ref[c0:c0 + cn, :]) * hsum).astype(BF)


def _lru_call(lx, lg, clx, clg, wts, batch):
    n, w = lx.shape
    s = n // batch
    c = clx.shape[0] // batch
    seq = lambda a, ln: a.reshape(batch, ln, w)
    lat = pl.BlockSpec((None, s, w), lambda b: (b, 0, 0))
    ctx = pl.BlockSpec((None, c, w), lambda b: (b, 0, 0))
    consts = [wts["conv_w"], wts["conv_b"], wts["w_gate"], wts["b_gate"], wts["lam"]]
    o, oc = pl.pallas_call(
        _lru_kernel,
        grid=(batch,),
        in_specs=[lat, lat, ctx, ctx] + [_const_spec(a.shape) for a in consts],
        out_specs=[lat, ctx],
        out_shape=[jax.ShapeDtypeStruct((batch, s, w), BF), jax.ShapeDtypeStruct((batch, c, w), BF)],
        scratch_shapes=[
            pltpu.VMEM((s + 2 * LRU_PAD, w), F32),
            pltpu.VMEM((2, s, w), F32),
            pltpu.VMEM((2, s, w), F32),
            pltpu.VMEM((2, c, w), F32),
            pltpu.VMEM((2, c, w), F32),
        ],
        compiler_params=_params("parallel"),
        name="rglru",
    )(seq(lx, s), seq(lg, s), seq(clx, c), seq(clg, c), *consts)
    return o.reshape(n, w), oc.reshape(batch * c, w)


def _out_kernel(h_ref, na_ref, mla_ref, lru_ref, gm_ref, shf_ref, scf_ref, gf_ref, g_ref,
                wo_ref, wg_ref, wu_ref, wd_ref, o_ref, *, hid_chunk):
    o_cat = jnp.concatenate([na_ref[...], mla_ref[...], lru_ref[...]], axis=-1)
    h1 = h_ref[...] + gm_ref[0] * _dot(o_cat, wo_ref[...])
    y = h1 * lax.rsqrt(jnp.mean(h1 * h1, axis=-1, keepdims=True) + EPS) * g_ref[...]
    y = (y * (1.0 + scf_ref[0]) + shf_ref[0]).astype(BF)
    hidden = wg_ref.shape[1]
    acc = jnp.zeros(h1.shape, F32)
    for c0 in range(0, hidden, hid_chunk):
        gate = _dot(y, wg_ref[:, c0:c0 + hid_chunk])
        up = _dot(y, wu_ref[:, c0:c0 + hid_chunk])
        acc = acc + _dot((_silu(gate) * up).astype(BF), wd_ref[c0:c0 + hid_chunk, :])
    o_ref[...] = h1 + gf_ref[0] * acc


def _out_call(h, na_o, mla_o, lru_o, mods, wts, tm, tiles_per_mod):
    n, d = h.shape
    if tiles_per_mod is None:
        mod_map = lambda i: (0, 0, 0)
    else:
        mod_map = lambda i: (i // tiles_per_mod, 0, 0)
    row = lambda w: pl.BlockSpec((tm, w), lambda i: (i, 0))
    mod = pl.BlockSpec((1, 1, d), mod_map)
    consts = [wts["g_ffn"], wts["w_out"], wts["ffn_gate"], wts["ffn_up"], wts["ffn_down"]]
    hidden = wts["ffn_gate"].shape[1]
    hid_chunk = hidden // 2 if (hidden // 2) % LANE == 0 else hidden
    return pl.pallas_call(
        functools.partial(_out_kernel, hid_chunk=hid_chunk),
        grid=(n // tm,),
        in_specs=[row(d), row(na_o.shape[1]), row(mla_o.shape[1]), row(lru_o.shape[1]), mod, mod, mod, mod]
                 + [_const_spec(a.shape) for a in consts],
        out_specs=row(d),
        out_shape=jax.ShapeDtypeStruct((n, d), F32),
        compiler_params=_params("parallel"),
        name="out_ffn",
    )(h, na_o, mla_o, lru_o, *mods, *consts)


def _rope_perm():
    j = np.arange(MLA_ROPE)
    first = (j % (MLA_ROPE // 2)) < MLA_ROPE // 4
    partner = np.where(first, j + MLA_ROPE // 4, j - MLA_ROPE // 4)
    sign = np.where(first, -1.0, 1.0).astype(np.float32)
    return partner, sign


def _rope_tables(s):
    t = np.arange(s)
    row = (t // GRID_W).astype(np.float32)
    col = (t % GRID_W).astype(np.float32)
    n_freq = MLA_ROPE // 4
    inv = jnp.asarray(ROPE_BASE, F32) ** (-jnp.arange(n_freq, dtype=F32) / n_freq)
    ar = jnp.asarray(row)[:, None] * inv
    ac = jnp.asarray(col)[:, None] * inv
    ang = jnp.concatenate([ar, ar, ac, ac], axis=-1)
    pad = LANE - MLA_NOPE - MLA_ROPE
    cos = jnp.concatenate([jnp.ones((s, MLA_NOPE), F32), jnp.cos(ang), jnp.zeros((s, pad), F32)], axis=-1)
    sin = jnp.concatenate([jnp.zeros((s, MLA_NOPE), F32), jnp.sin(ang), jnp.zeros((s, pad), F32)], axis=-1)
    return cos, sin


def _lane_tile(nope, rope):
    lead = nope.shape[:-1]
    pad = jnp.zeros(lead + (LANE - MLA_NOPE - MLA_ROPE,), nope.dtype)
    return jnp.concatenate([nope, rope, pad], axis=-1)


def _layer_weights(l, p):
    partner, sign = _rope_perm()
    d = p["w_in"].shape[1]
    w_in = p["w_in"][l]
    cols = np.cumsum((NA_WIDTH, NA_WIDTH, NA_WIDTH, MLA_Q_RANK, MLA_KV_RANK, MLA_ROPE, LRU_WIDTH, LRU_WIDTH))
    w_kr = w_in[:, cols[4]:cols[5]]
    zn = jnp.zeros((d, MLA_NOPE), F32)
    w_all = jnp.concatenate([
        w_in[:, :cols[4]], _lane_tile(zn, w_kr), _lane_tile(zn, w_kr[:, partner]), w_in[:, cols[5]:]], axis=-1)
    assert w_all.shape[1] == W_ALL

    wq = p["mla_w_qb"][l].reshape(MLA_Q_RANK, MLA_HEADS, MLA_NOPE + MLA_ROPE)
    wq_n, wq_r = wq[..., :MLA_NOPE], wq[..., MLA_NOPE:]
    w_qb = jnp.concatenate([
        _lane_tile(wq_n, wq_r).reshape(MLA_Q_RANK, MLA_QK_W),
        _lane_tile(jnp.zeros_like(wq_n), wq_r[..., partner]).reshape(MLA_Q_RANK, MLA_QK_W)], axis=-1)

    wkv = p["mla_w_kvb"][l].reshape(MLA_KV_RANK, MLA_HEADS, MLA_NOPE + MLA_V)
    wk_n, wv = wkv[..., :MLA_NOPE], wkv[..., MLA_NOPE:]
    w_kvb = jnp.concatenate([
        _lane_tile(wk_n, jnp.zeros(wk_n.shape[:-1] + (MLA_ROPE,), F32)).reshape(MLA_KV_RANK, MLA_QK_W),
        wv.reshape(MLA_KV_RANK, MLA_HEADS * MLA_V)], axis=-1)

    qg, kg = p["mla_q_g"][l], p["mla_k_g"][l]
    z_n, z_r = jnp.zeros((MLA_NOPE,), F32), jnp.zeros((MLA_ROPE,), F32)
    sgn = jnp.asarray(sign)
    g_mla = jnp.stack([
        _lane_tile(qg[:MLA_NOPE], qg[MLA_NOPE:]) * MLA_SCALE,
        _lane_tile(z_n, sgn * qg[MLA_NOPE:][partner]) * MLA_SCALE,
        _lane_tile(kg[:MLA_NOPE], z_r),
        _lane_tile(z_n, kg[MLA_NOPE:]),
        _lane_tile(z_n, sgn * kg[MLA_NOPE:][partner]),
        jnp.zeros((LANE,), F32), jnp.zeros((LANE,), F32), jnp.zeros((LANE,), F32)])

    g_na = jnp.stack([jnp.tile(p["na_q_g"][l], NA_HEADS) * NA_SCALE, jnp.tile(p["na_k_g"][l], NA_HEADS)])

    def dense(wh):
        out = jnp.zeros((LRU_WIDTH, LRU_WIDTH), F32)
        for hd in range(LRU_HEADS):
            sl = slice(hd * LRU_BLOCK, (hd + 1) * LRU_BLOCK)
            out = out.at[sl, sl].set(wh[hd])
        return out
    wa, wx = p["lru_w_a"][l], p["lru_w_x"][l]
    w_gate = jnp.concatenate([dense(wa[0]), dense(wa[1]), dense(wx[0]), dense(wx[1])], axis=-1)
    b_gate = jnp.concatenate([p["lru_b_a"][l, 0], p["lru_b_a"][l, 1], p["lru_b_x"][l, 0], p["lru_b_x"][l, 1]])

    return {
        "g_mix": p["norm_mix_g"][l][None], "g_ffn": p["norm_ffn_g"][l][None],
        "w_all": w_all.astype(BF), "w_qb": w_qb.astype(BF), "w_kvb": w_kvb.astype(BF),
        "g_na": g_na, "g_cq": p["mla_cq_g"][l][None], "g_ckv": p["mla_ckv_g"][l][None], "g_mla": g_mla,
        "conv_w": p["lru_conv_w"][l], "conv_b": p["lru_conv_b"][l][None],
        "w_gate": w_gate.astype(BF), "b_gate": b_gate[None], "lam": p["lru_lambda"][l].reshape(1, 2 * LRU_WIDTH),
        "w_out": p["w_out"][l].astype(BF), "ffn_gate": p["ffn_w_gate"][l].astype(BF),
        "ffn_up": p["ffn_w_up"][l].astype(BF), "ffn_down": p["ffn_w_down"][l].astype(BF),
    }


def _seg_mean_matrices():
    i = np.arange(LANE)
    e_na = (i[:, None] // NA_HEAD_DIM == i[None, :] // NA_HEAD_DIM).astype(np.float32) / NA_HEAD_DIM
    nope = i < MLA_NOPE
    rope = (i >= MLA_NOPE) & (i < MLA_NOPE + MLA_ROPE)
    e_mla = ((nope[:, None] & nope[None, :]).astype(np.float32) / MLA_NOPE
             + (rope[:, None] & rope[None, :]).astype(np.float32) / MLA_ROPE)
    return jnp.asarray(e_na, BF), jnp.asarray(e_mla, BF)


def kernel(x, c, ctx, c_ctx, ada_w, ada_b, norm_mix_g, norm_ffn_g, w_in, na_q_g, na_k_g, na_rpb, mla_cq_g, mla_w_qb, mla_ckv_g, mla_w_kvb, mla_q_g, mla_k_g, lru_conv_w, lru_conv_b, lru_w_a, lru_b_a, lru_w_x, lru_b_x, lru_lambda, w_out, ffn_w_gate, ffn_w_up, ffn_w_down):
    batch, s, d = x.shape
    cl = ctx.shape[1]
    depth = ada_w.shape[0]
    rows = s // GRID_W
    assert s % GRID_W == 0 and rows >= NA_WIN_H
    p = dict(norm_mix_g=norm_mix_g, norm_ffn_g=norm_ffn_g, w_in=w_in, na_q_g=na_q_g, na_k_g=na_k_g,
             mla_cq_g=mla_cq_g, mla_w_qb=mla_w_qb, mla_ckv_g=mla_ckv_g, mla_w_kvb=mla_w_kvb, mla_q_g=mla_q_g,
             mla_k_g=mla_k_g, lru_conv_w=lru_conv_w, lru_conv_b=lru_conv_b, lru_w_a=lru_w_a, lru_b_a=lru_b_a,
             lru_w_x=lru_w_x, lru_b_x=lru_b_x, lru_lambda=lru_lambda, w_out=w_out, ffn_w_gate=ffn_w_gate,
             ffn_w_up=ffn_w_up, ffn_w_down=ffn_w_down)

    tm = 512 if s % 512 == 0 else GRID_W * NA_WIN_H
    tmc = tm if (batch * cl) % tm == 0 else cl
    tq = 256

    nmod = -(-(batch + 1) // SUBLANE) * SUBLANE
    cvec = jnp.concatenate([c, c_ctx[None], jnp.zeros((nmod - batch - 1, d), F32)], axis=0)
    mods = _ada_call(cvec, ada_w, ada_b)

    cos, sin = _rope_tables(s)
    cos_c = jnp.concatenate([jnp.ones((tmc, MLA_NOPE + MLA_ROPE), F32),
                             jnp.zeros((tmc, LANE - MLA_NOPE - MLA_ROPE), F32)], axis=-1)
    sin_c = jnp.zeros((tmc, LANE), F32)
    e_na, e_mla = _seg_mean_matrices()

    h = x.reshape(batch * s, d)
    hc = ctx.reshape(batch * cl, d)
    for l in range(depth):
        last = l == depth - 1
        wts = _layer_weights(l, p)
        wts["e_na"], wts["e_mla"] = e_na, e_mla
        m = mods[l]
        lat_mods = [m[:batch, i * d:(i + 1) * d].reshape(batch, 1, d) for i in range(6)]
        ctx_mods = [m[batch:batch + 1, i * d:(i + 1) * d].reshape(1, 1, d) for i in range(6)]

        q, k, v, mq, mk, mv, lx, lg = _in_call(h, lat_mods[0], lat_mods[1], wts, cos, sin, tm, s // tm)
        cq_, ck, cv, cmq, cmk, cmv, clx, clg = _in_call(hc, ctx_mods[0], ctx_mods[1], wts, cos_c, sin_c, tmc, None)

        na_o = _na_call(q, k, v, ck, cv, _na_bias_table(na_rpb[l], rows), batch)
        mla_o = _mla_call(mq, mk, mv, cmk, cmv, batch, tq)
        lru_o, lru_c = _lru_call(lx, lg, clx, clg, wts, batch)
        h = _out_call(h, na_o, mla_o, lru_o, [lat_mods[2], lat_mods[3], lat_mods[4], lat_mods[5]], wts, tm, s // tm)

        if not last:
            na_c, mla_c = _ctx_attn_call(cq_, ck, cv, cmq, cmk, cmv, batch)
            hc = _out_call(hc, na_c, mla_c, lru_c, [ctx_mods[2], ctx_mods[3], ctx_mods[4], ctx_mods[5]], wts, tmc, None)
    return h.reshape(batch, s, d)
```

```python
import functools
import math

import numpy as np
import jax
import jax.numpy as jnp
from jax import lax
from jax.experimental import pallas as pl
from jax.experimental.pallas import tpu as pltpu

GRID_W = 64
NA_HEADS = 6
NA_HEAD_DIM = 64
NA_WIDTH = NA_HEADS * NA_HEAD_DIM
NA_WIN_H = 8
NA_WIN_W = 16
MLA_HEADS = 6
MLA_Q_RANK = 256
MLA_KV_RANK = 128
MLA_NOPE = 64
MLA_ROPE = 32
MLA_V = 64
LRU_WIDTH = 256
LRU_HEADS = 4
LRU_BLOCK = LRU_WIDTH // LRU_HEADS
LRU_CONV_W = 4
LRU_C = 8.0
ROPE_BASE = 10000.0
EPS = 1e-6
LOG2E = math.log2(math.e)
NA_SCALE = NA_HEAD_DIM ** -0.5 * LOG2E
MLA_SCALE = (MLA_NOPE + MLA_ROPE) ** -0.5 * LOG2E

LANE = 128
SUBLANE = 8
MLA_QK_W = MLA_HEADS * LANE
NEG_BIG = -1e30
VMEM_LIMIT = 56 * 1024 * 1024

NA_QROWS = 4
NA_KROWS = NA_QROWS + NA_WIN_H
NA_CLASSES = 3

C_Q, C_K, C_V = 0, NA_WIDTH, 2 * NA_WIDTH
C_CQ = 3 * NA_WIDTH
C_CKV = C_CQ + MLA_Q_RANK
C_KR = C_CKV + MLA_KV_RANK
C_KRP = C_KR + LANE
C_LX = C_KRP + LANE
C_LG = C_LX + LRU_WIDTH
W_ALL = C_LG + LRU_WIDTH

BF = jnp.bfloat16
F32 = jnp.float32


def _dot(a, b):
    return jnp.dot(a, b, preferred_element_type=F32)


def _dot_nt(a, b):
    return lax.dot_general(a, b, (((1,), (1,)), ((), ())), preferred_element_type=F32)


def _sigmoid(x):
    return 0.5 + 0.5 * jnp.tanh(0.5 * x)


def _silu(x):
    hx = 0.5 * x
    return hx + hx * jnp.tanh(hx)


def _gelu_tanh(x):
    return 0.5 * x * (1.0 + jnp.tanh(0.7978845608028654 * (x + 0.044715 * (x * x * x))))


def _params(*sem):
    return pltpu.CompilerParams(dimension_semantics=sem, vmem_limit_bytes=VMEM_LIMIT)


def _const_spec(shape):
    nd = len(shape)
    return pl.BlockSpec(shape, lambda *_: (0,) * nd, pipeline_mode=pl.Buffered(1))


def _layer_spec(arr, l):
    nd = arr.ndim
    return pl.BlockSpec((None,) + arr.shape[1:], lambda *_: (l,) + (0,) * (nd - 1), pipeline_mode=pl.Buffered(1))


def _mod_spec(mods, l, which, row_map):
    d = mods.shape[-1]
    return pl.BlockSpec((None, None, None, 1, d), lambda i: (l, row_map(i), which, 0, 0))


def _ada_kernel(c_ref, w_ref, b_ref, o_ref):
    s = _silu(c_ref[...]).astype(BF)
    o_ref[...] = _dot(s, w_ref[...].astype(BF)) + b_ref[...]


def _ada_call(cvec, ada_w, ada_b):
    depth, d, d6 = ada_w.shape
    r = cvec.shape[0]
    tn = 1024
    return pl.pallas_call(
        _ada_kernel,
        grid=(depth, d6 // tn),
        in_specs=[
            pl.BlockSpec((r, d), lambda l, j: (0, 0)),
            pl.BlockSpec((None, d, tn), lambda l, j: (l, 0, j)),
            pl.BlockSpec((None, 1, tn), lambda l, j: (l, 0, j)),
        ],
        out_specs=pl.BlockSpec((None, r, tn), lambda l, j: (l, 0, j)),
        out_shape=jax.ShapeDtypeStruct((depth, r, d6), F32),
        compiler_params=_params("parallel", "parallel"),
        name="adaln",
    )(cvec, ada_w, ada_b.reshape(depth, 1, d6))


def _in_kernel(h_ref, sh_ref, sc_ref, g_ref, wall_ref, wqb_ref, wkvb_ref, gna_ref, gcq_ref, gckv_ref,
               gm_ref, ena_ref, emla_ref, cos_ref, sin_ref,
               q_ref, k_ref, v_ref, mq_ref, mk_ref, mv_ref, lx_ref, lg_ref):
    x = h_ref[...]
    y = x * lax.rsqrt(jnp.mean(x * x, axis=-1, keepdims=True) + EPS) * g_ref[...]
    y = y * (1.0 + sc_ref[...]) + sh_ref[...]
    u = _dot(y.astype(BF), wall_ref[...])

    ena = ena_ref[...]
    for p in range(NA_WIDTH // LANE):
        lo = p * LANE
        uq = u[:, C_Q + lo:C_Q + lo + LANE]
        msq = _dot((uq * uq).astype(BF), ena)
        q_ref[:, lo:lo + LANE] = (uq * lax.rsqrt(msq + EPS) * gna_ref[0:1, lo:lo + LANE]).astype(BF)
        uk = u[:, C_K + lo:C_K + lo + LANE]
        msk = _dot((uk * uk).astype(BF), ena)
        k_ref[:, lo:lo + LANE] = (uk * lax.rsqrt(msk + EPS) * gna_ref[1:2, lo:lo + LANE]).astype(BF)
    v_ref[...] = u[:, C_V:C_V + NA_WIDTH].astype(BF)

    emla = emla_ref[...]
    cos = cos_ref[...]
    sin = sin_ref[...]
    cq = u[:, C_CQ:C_CQ + MLA_Q_RANK]
    cqn = cq * lax.rsqrt(jnp.mean(cq * cq, axis=-1, keepdims=True) + EPS) * gcq_ref[...]
    qq = _dot(cqn.astype(BF), wqb_ref[...])
    a_q = gm_ref[0:1, :] * cos
    b_q = gm_ref[1:2, :] * sin
    for hd in range(MLA_HEADS):
        lo = hd * LANE
        qr = qq[:, lo:lo + LANE]
        qp = qq[:, MLA_QK_W + lo:MLA_QK_W + lo + LANE]
        ms = _dot((qr * qr).astype(BF), emla)
        mq_ref[:, lo:lo + LANE] = (lax.rsqrt(ms + EPS) * (qr * a_q + qp * b_q)).astype(BF)

    ckv = u[:, C_CKV:C_CKV + MLA_KV_RANK]
    ckvn = ckv * lax.rsqrt(jnp.mean(ckv * ckv, axis=-1, keepdims=True) + EPS) * gckv_ref[...]
    kv = _dot(ckvn.astype(BF), wkvb_ref[...])
    krr = u[:, C_KR:C_KR + LANE]
    krp = u[:, C_KRP:C_KRP + LANE]
    msr = _dot((krr * krr).astype(BF), emla)
    kr = lax.rsqrt(msr + EPS) * (krr * (gm_ref[3:4, :] * cos) + krp * (gm_ref[4:5, :] * sin))
    for hd in range(MLA_HEADS):
        lo = hd * LANE
        kn = kv[:, lo:lo + LANE]
        ms = _dot((kn * kn).astype(BF), emla)
        mk_ref[:, lo:lo + LANE] = (kn * lax.rsqrt(ms + EPS) * gm_ref[2:3, :] + kr).astype(BF)
    mv_ref[...] = kv[:, MLA_QK_W:].astype(BF)

    lx_ref[...] = u[:, C_LX:C_LX + LRU_WIDTH]
    lg_ref[...] = u[:, C_LG:C_LG + LRU_WIDTH]


def _in_call(h, mods, row_map, l, wts, cos, sin, tm):
    n, d = h.shape
    tab_tiles = cos.shape[0] // tm
    row = lambda w: pl.BlockSpec((tm, w), lambda i: (i, 0))
    tab = pl.BlockSpec((tm, LANE), lambda i: (i % tab_tiles, 0))
    stacked = [wts["g_mix"], wts["w_all"], wts["w_qb"], wts["w_kvb"], wts["g_na"], wts["g_cq"], wts["g_ckv"],
               wts["g_mla"]]
    consts = [wts["e_na"], wts["e_mla"]]
    out_w = [(NA_WIDTH, BF), (NA_WIDTH, BF), (NA_WIDTH, BF), (MLA_QK_W, BF), (MLA_QK_W, BF),
             (MLA_HEADS * MLA_V, BF), (LRU_WIDTH, F32), (LRU_WIDTH, F32)]
    return pl.pallas_call(
        _in_kernel,
        grid=(n // tm,),
        in_specs=[row(d), _mod_spec(mods, l, 0, row_map), _mod_spec(mods, l, 1, row_map)]
                 + [_layer_spec(a, l) for a in stacked] + [_const_spec(a.shape) for a in consts] + [tab, tab],
        out_specs=[row(w) for w, _ in out_w],
        out_shape=[jax.ShapeDtypeStruct((n, w), dt) for w, dt in out_w],
        compiler_params=_params("parallel"),
        name="in_proj",
    )(h, mods, mods, *stacked, *consts, cos, sin)


def _softmax_pv(s_a, s_b, v_a, v_b):
    m = jnp.maximum(jnp.max(s_a, axis=-1, keepdims=True), jnp.max(s_b, axis=-1, keepdims=True))
    p_a = jnp.exp2(s_a - m)
    p_b = jnp.exp2(s_b - m)
    l = jnp.sum(p_a, axis=-1, keepdims=True) + jnp.sum(p_b, axis=-1, keepdims=True)
    o = _dot(p_a.astype(BF), v_a) + _dot(p_b.astype(BF), v_b)
    return o / l


def _na_block_geometry(rows):
    nblk = rows // NA_QROWS
    half = NA_WIN_H // 2
    ks = np.clip(np.arange(nblk) * NA_QROWS - half, 0, rows - NA_KROWS)
    pats = []
    for j in range(nblk):
        r = j * NA_QROWS + np.arange(NA_QROWS)
        rs = np.clip(r - half, 0, rows - NA_WIN_H)
        assert ks[j] <= rs.min() and rs.max() + NA_WIN_H <= ks[j] + NA_KROWS
        pats.append(tuple(zip((r - ks[j]).tolist(), (rs - ks[j]).tolist())))
    cls = [0 if j == 0 else (2 if j == nblk - 1 else 1) for j in range(nblk)]
    by_cls = {}
    for j in range(nblk):
        assert by_cls.setdefault(cls[j], pats[j]) == pats[j]
    return [by_cls[c] for c in range(NA_CLASSES)]


def _na_kernel(q_ref, k_ref, v_ref, kc_ref, vc_ref, bias_ref, o_ref, *, rows):
    j = pl.program_id(1)
    nblk = rows // NA_QROWS
    ks = jnp.clip(j * NA_QROWS - NA_WIN_H // 2, 0, rows - NA_KROWS)
    start = pl.multiple_of(ks * GRID_W, GRID_W)
    cls = jnp.where(j == 0, 0, jnp.where(j == nblk - 1, 2, 1))
    nq = NA_QROWS * GRID_W
    nk = NA_KROWS * GRID_W
    first_half = lax.broadcasted_iota(jnp.int32, (nq, LANE), 1) < NA_HEAD_DIM
    for p in range(NA_WIDTH // LANE):
        lo = p * LANE
        q2 = q_ref[:, lo:lo + LANE]
        k2 = k_ref[pl.ds(start, nk), lo:lo + LANE]
        v2 = v_ref[pl.ds(start, nk), lo:lo + LANE]
        kc2 = kc_ref[:, lo:lo + LANE]
        vc2 = vc_ref[:, lo:lo + LANE]
        outs = []
        for hh in range(2):
            qm = jnp.where(first_half if hh == 0 else ~first_half, q2, jnp.zeros_like(q2))
            s_w = _dot_nt(qm, k2) + bias_ref[cls, 2 * p + hh]
            s_c = _dot_nt(qm, kc2)
            outs.append(_softmax_pv(s_w, s_c, v2, vc2))
        o_ref[:, lo:lo + LANE] = jnp.where(first_half, outs[0], outs[1]).astype(BF)


def _na_call(q, k, v, kc, vc, bias, l, batch):
    n, w = q.shape
    s = n // batch
    c = kc.shape[0] // batch
    rows = s // GRID_W
    q3, k3, v3 = (a.reshape(batch, s, w) for a in (q, k, v))
    kc3, vc3 = (a.reshape(batch, c, w) for a in (kc, vc))
    nq = NA_QROWS * GRID_W
    out = pl.pallas_call(
        functools.partial(_na_kernel, rows=rows),
        grid=(batch, rows // NA_QROWS),
        in_specs=[
            pl.BlockSpec((None, nq, w), lambda b, j: (b, j, 0)),
            pl.BlockSpec((None, s, w), lambda b, j: (b, 0, 0)),
            pl.BlockSpec((None, s, w), lambda b, j: (b, 0, 0)),
            pl.BlockSpec((None, c, w), lambda b, j: (b, 0, 0)),
            pl.BlockSpec((None, c, w), lambda b, j: (b, 0, 0)),
            _layer_spec(bias, l),
        ],
        out_specs=pl.BlockSpec((None, nq, w), lambda b, j: (b, j, 0)),
        out_shape=jax.ShapeDtypeStruct((batch, s, w), BF),
        compiler_params=_params("parallel", "arbitrary"),
        name="nbr_attn",
    )(q3, k3, v3, kc3, vc3, bias)
    return out.reshape(n, w)


def _na_bias_tables(rpb, rows):
    depth = rpb.shape[0]
    qc = np.arange(GRID_W)
    kcol = np.arange(GRID_W)
    cs = np.clip(qc - NA_WIN_W // 2, 0, GRID_W - NA_WIN_W)
    ok = (kcol[None, :] >= cs[:, None]) & (kcol[None, :] < cs[:, None] + NA_WIN_W)
    col_rel = np.clip(kcol[None, :] - qc[:, None], 1 - NA_WIN_W, NA_WIN_W - 1) + NA_WIN_W - 1
    t = jnp.where(ok, rpb.astype(F32)[:, :, :, col_rel] * LOG2E, NEG_BIG)
    neg = jnp.full((depth, NA_HEADS, GRID_W, GRID_W), NEG_BIG, F32)
    classes = []
    for pat in _na_block_geometry(rows):
        qrows = []
        for qo, wo in pat:
            blocks = [t[:, :, kr - qo + NA_WIN_H - 1] if wo <= kr < wo + NA_WIN_H else neg for kr in range(NA_KROWS)]
            qrows.append(jnp.concatenate(blocks, axis=-1))
        classes.append(jnp.concatenate(qrows, axis=-2))
    return jnp.stack(classes, axis=1)


def _mla_kernel(q_ref, k_ref, v_ref, kc_ref, vc_ref, o_ref):
    tq = q_ref.shape[0]
    first_half = lax.broadcasted_iota(jnp.int32, (tq, LANE), 1) < MLA_V
    for p in range(MLA_HEADS // 2):
        lo = p * LANE
        v2 = v_ref[:, lo:lo + LANE]
        vc2 = vc_ref[:, lo:lo + LANE]
        outs = []
        for hh in range(2):
            hl = (2 * p + hh) * LANE
            qh = q_ref[:, hl:hl + LANE]
            s_l = _dot_nt(qh, k_ref[:, hl:hl + LANE])
            s_c = _dot_nt(qh, kc_ref[:, hl:hl + LANE])
            outs.append(_softmax_pv(s_l, s_c, v2, vc2))
        o_ref[:, lo:lo + LANE] = jnp.where(first_half, outs[0], outs[1]).astype(BF)


def _mla_call(q, k, v, kc, vc, batch, tq):
    n, wq = q.shape
    wv = v.shape[1]
    s = n // batch
    c = kc.shape[0] // batch
    q3, k3 = q.reshape(batch, s, wq), k.reshape(batch, s, wq)
    v3 = v.reshape(batch, s, wv)
    kc3, vc3 = kc.reshape(batch, c, wq), vc.reshape(batch, c, wv)
    out = pl.pallas_call(
        _mla_kernel,
        grid=(batch, s // tq),
        in_specs=[
            pl.BlockSpec((None, tq, wq), lambda b, i: (b, i, 0)),
            pl.BlockSpec((None, s, wq), lambda b, i: (b, 0, 0)),
            pl.BlockSpec((None, s, wv), lambda b, i: (b, 0, 0)),
            pl.BlockSpec((None, c, wq), lambda b, i: (b, 0, 0)),
            pl.BlockSpec((None, c, wv), lambda b, i: (b, 0, 0)),
        ],
        out_specs=pl.BlockSpec((None, tq, wv), lambda b, i: (b, i, 0)),
        out_shape=jax.ShapeDtypeStruct((batch, s, wv), BF),
        compiler_params=_params("parallel", "arbitrary"),
        name="latent_attn",
    )(q3, k3, v3, kc3, vc3)
    return out.reshape(n, wv)


def _softmax_pv1(s, v):
    m = jnp.max(s, axis=-1, keepdims=True)
    p = jnp.exp2(s - m)
    l = jnp.sum(p, axis=-1, keepdims=True)
    return _dot(p.astype(BF), v) / l


def _ctx_attn_kernel(q_ref, k_ref, v_ref, mq_ref, mk_ref, mv_ref, ona_ref, omla_ref):
    c = q_ref.shape[0]
    first_half = lax.broadcasted_iota(jnp.int32, (c, LANE), 1) < NA_HEAD_DIM
    for p in range(NA_WIDTH // LANE):
        lo = p * LANE
        q2 = q_ref[:, lo:lo + LANE]
        k2 = k_ref[:, lo:lo + LANE]
        v2 = v_ref[:, lo:lo + LANE]
        outs = []
        for hh in range(2):
            qm = jnp.where(first_half if hh == 0 else ~first_half, q2, jnp.zeros_like(q2))
            outs.append(_softmax_pv1(_dot_nt(qm, k2), v2))
        ona_ref[:, lo:lo + LANE] = jnp.where(first_half, outs[0], outs[1]).astype(BF)
    for p in range(MLA_HEADS // 2):
        lo = p * LANE
        v2 = mv_ref[:, lo:lo + LANE]
        outs = []
        for hh in range(2):
            hl = (2 * p + hh) * LANE
            outs.append(_softmax_pv1(_dot_nt(mq_ref[:, hl:hl + LANE], mk_ref[:, hl:hl + LANE]), v2))
        omla_ref[:, lo:lo + LANE] = jnp.where(first_half, outs[0], outs[1]).astype(BF)


def _ctx_attn_call(q, k, v, mq, mk, mv, batch):
    nc = q.shape[0]
    c = nc // batch
    ins = [q, k, v, mq, mk, mv]
    ins3 = [a.reshape(batch, c, a.shape[1]) for a in ins]
    spec = lambda w: pl.BlockSpec((None, c, w), lambda b: (b, 0, 0))
    ona, omla = pl.pallas_call(
        _ctx_attn_kernel,
        grid=(batch,),
        in_specs=[spec(a.shape[1]) for a in ins],
        out_specs=[spec(NA_WIDTH), spec(MLA_HEADS * MLA_V)],
        out_shape=[jax.ShapeDtypeStruct((batch, c, NA_WIDTH), BF),
                   jax.ShapeDtypeStruct((batch, c, MLA_HEADS * MLA_V), BF)],
        compiler_params=_params("parallel"),
        name="ctx_attn",
    )(*ins3)
    return ona.reshape(nc, NA_WIDTH), omla.reshape(nc, MLA_HEADS * MLA_V)


LRU_PAD = SUBLANE
LRU_CHUNK = 512


def _lru_coeffs(x_ref, n, xpad_ref, a_ref, u_ref, cw_ref, cb_ref, wg_ref, bg_ref, sp):
    w = LRU_WIDTH
    zeros = jnp.zeros((LRU_PAD, w), F32)
    xpad_ref[0:LRU_PAD, :] = zeros
    xpad_ref[LRU_PAD:LRU_PAD + n, :] = x_ref[...]
    xpad_ref[LRU_PAD + n:2 * LRU_PAD + n, :] = zeros
    left = (LRU_CONV_W - 1) // 2
    for c0 in range(0, n, LRU_CHUNK):
        cn = min(LRU_CHUNK, n - c0)
        xc = jnp.zeros((cn, w), F32) + cb_ref[...]
        for j in range(LRU_CONV_W):
            off = LRU_PAD + c0 + j - left
            xc = xc + cw_ref[j:j + 1, :] * xpad_ref[off:off + cn, :]
        g = _dot(xc.astype(BF), wg_ref[...]) + bg_ref[...]
        for dr in range(2):
            log_a = (-LRU_C) * _sigmoid(g[:, dr * w:(dr + 1) * w]) * sp[:, dr * w:(dr + 1) * w]
            a = jnp.exp(log_a)
            gx = _sigmoid(g[:, (2 + dr) * w:(3 + dr) * w])
            a_ref[dr, c0:c0 + cn, :] = a
            u_ref[dr, c0:c0 + cn, :] = jnp.sqrt(1.0 - a * a) * (gx * xc)


def _lru_scan(n, a_ref, u_ref, hf0, hb0):
    w = LRU_WIDTH
    nchunk = n // SUBLANE
    rowi = lax.broadcasted_iota(jnp.int32, (SUBLANE, w), 0)

    def body(c, carry):
        hf, hb = carry
        rf = pl.multiple_of(c * SUBLANE, SUBLANE)
        rb = pl.multiple_of((nchunk - 1 - c) * SUBLANE, SUBLANE)
        af = a_ref[0, pl.ds(rf, SUBLANE), :]
        uf = u_ref[0, pl.ds(rf, SUBLANE), :]
        ab = a_ref[1, pl.ds(rb, SUBLANE), :]
        ub = u_ref[1, pl.ds(rb, SUBLANE), :]
        for s in (1, 2, 4):
            mf = rowi >= s
            uf = jnp.where(mf, af * pltpu.roll(uf, s, 0) + uf, uf)
            af = jnp.where(mf, af * pltpu.roll(af, s, 0), af)
            mb = rowi < SUBLANE - s
            ub = jnp.where(mb, ab * pltpu.roll(ub, SUBLANE - s, 0) + ub, ub)
            ab = jnp.where(mb, ab * pltpu.roll(ab, SUBLANE - s, 0), ab)
        hfc = af * hf + uf
        hbc = ab * hb + ub
        u_ref[0, pl.ds(rf, SUBLANE), :] = hfc
        u_ref[1, pl.ds(rb, SUBLANE), :] = hbc
        return hfc[SUBLANE - 1:SUBLANE, :], hbc[0:1, :]

    return lax.fori_loop(0, nchunk, body, (hf0, hb0))


def _lru_kernel(lx_ref, lg_ref, clx_ref, clg_ref, cw_ref, cb_ref, wg_ref, bg_ref, lam_ref,
                o_ref, oc_ref, xpad_ref, a_ref, u_ref, ac_ref, uc_ref):
    n = lx_ref.shape[0]
    nc = clx_ref.shape[0]
    nlam = -lam_ref[...]
    sp = jnp.maximum(nlam, 0.0) + jnp.log(1.0 + jnp.exp(-jnp.abs(nlam)))
    zero = jnp.zeros((1, LRU_WIDTH), F32)
    _lru_coeffs(clx_ref, nc, xpad_ref, ac_ref, uc_ref, cw_ref, cb_ref, wg_ref, bg_ref, sp)
    hf0, hb0 = _lru_scan(nc, ac_ref, uc_ref, zero, zero)
    oc_ref[...] = (_gelu_tanh(clg_ref[...]) * (uc_ref[0] + uc_ref[1])).astype(BF)
    _lru_coeffs(lx_ref, n, xpad_ref, a_ref, u_ref, cw_ref, cb_ref, wg_ref, bg_ref, sp)
    _lru_scan(n, a_ref, u_ref, hf0, hb0)
    for c0 in range(0, n, LRU_CHUNK):
        cn = min(LRU_CHUNK, n - c0)
        hsum = u_ref[0, c0:c0 + cn, :] + u_ref[1, c0:c0 + cn, :]
        o_ref[c0:c0 + cn, :] = (_gelu_tanh(lg_ref[c0:c0 + cn, :]) * hsum).astype(BF)


def _lru_call(lx, lg, clx, clg, l, wts, batch):
    n, w = lx.shape
    s = n // batch
    c = clx.shape[0] // batch
    seq = lambda a, ln: a.reshape(batch, ln, w)
    lat = pl.BlockSpec((None, s, w), lambda b: (b, 0, 0))
    ctx = pl.BlockSpec((None, c, w), lambda b: (b, 0, 0))
    stacked = [wts["conv_w"], wts["conv_b"], wts["w_gate"], wts["b_gate"], wts["lam"]]
    o, oc = pl.pallas_call(
        _lru_kernel,
        grid=(batch,),
        in_specs=[lat, lat, ctx, ctx] + [_layer_spec(a, l) for a in stacked],
        out_specs=[lat, ctx],
        out_shape=[jax.ShapeDtypeStruct((batch, s, w), BF), jax.ShapeDtypeStruct((batch, c, w), BF)],
        scratch_shapes=[
            pltpu.VMEM((s + 2 * LRU_PAD, w), F32),
            pltpu.VMEM((2, s, w), F32),
            pltpu.VMEM((2, s, w), F32),
            pltpu.VMEM((2, c, w), F32),
            pltpu.VMEM((2, c, w), F32),
        ],
        compiler_params=_params("parallel"),
        name="rglru",
    )(seq(lx, s), seq(lg, s), seq(clx, c), seq(clg, c), *stacked)
    return o.reshape(n, w), oc.reshape(batch * c, w)


def _out_kernel(h_ref, na_ref, mla_ref, lru_ref, gm_ref, shf_ref, scf_ref, gf_ref, g_ref,
                wo_ref, wg_ref, wu_ref, wd_ref, o_ref, *, hid_chunk):
    o_cat = jnp.concatenate([na_ref[...], mla_ref[...], lru_ref[...]], axis=-1)
    h1 = h_ref[...] + gm_ref[...] * _dot(o_cat, wo_ref[...])
    y = h1 * lax.rsqrt(jnp.mean(h1 * h1, axis=-1, keepdims=True) + EPS) * g_ref[...]
    y = (y * (1.0 + scf_ref[...]) + shf_ref[...]).astype(BF)
    hidden = wg_ref.shape[1]
    acc = jnp.zeros(h1.shape, F32)
    for c0 in range(0, hidden, hid_chunk):
        gate = _dot(y, wg_ref[:, c0:c0 + hid_chunk])
        up = _dot(y, wu_ref[:, c0:c0 + hid_chunk])
        acc = acc + _dot((_silu(gate) * up).astype(BF), wd_ref[c0:c0 + hid_chunk, :])
    o_ref[...] = h1 + gf_ref[...] * acc


def _out_call(h, na_o, mla_o, lru_o, mods, row_map, l, wts, tm):
    n, d = h.shape
    row = lambda w: pl.BlockSpec((tm, w), lambda i: (i, 0))
    stacked = [wts["g_ffn"], wts["w_out"], wts["ffn_gate"], wts["ffn_up"], wts["ffn_down"]]
    hidden = wts["ffn_gate"].shape[-1]
    hid_chunk = hidden // 2 if (hidden // 2) % LANE == 0 else hidden
    return pl.pallas_call(
        functools.partial(_out_kernel, hid_chunk=hid_chunk),
        grid=(n // tm,),
        in_specs=[row(d), row(na_o.shape[1]), row(mla_o.shape[1]), row(lru_o.shape[1])]
                 + [_mod_spec(mods, l, which, row_map) for which in (2, 3, 4, 5)]
                 + [_layer_spec(a, l) for a in stacked],
        out_specs=row(d),
        out_shape=jax.ShapeDtypeStruct((n, d), F32),
        compiler_params=_params("parallel"),
        name="out_ffn",
    )(h, na_o, mla_o, lru_o, mods, mods, mods, mods, *stacked)


def _rope_perm():
    j = np.arange(MLA_ROPE)
    first = (j % (MLA_ROPE // 2)) < MLA_ROPE // 4
    partner = np.where(first, j + MLA_ROPE // 4, j - MLA_ROPE // 4)
    sign = np.where(first, -1.0, 1.0).astype(np.float32)
    return partner, sign


def _rope_tables(s):
    t = np.arange(s)
    row = (t // GRID_W).astype(np.float32)
    col = (t % GRID_W).astype(np.float32)
    n_freq = MLA_ROPE // 4
    inv = jnp.asarray(ROPE_BASE, F32) ** (-jnp.arange(n_freq, dtype=F32) / n_freq)
    ar = jnp.asarray(row)[:, None] * inv
    ac = jnp.asarray(col)[:, None] * inv
    ang = jnp.concatenate([ar, ar, ac, ac], axis=-1)
    pad = LANE - MLA_NOPE - MLA_ROPE
    cos = jnp.concatenate([jnp.ones((s, MLA_NOPE), F32), jnp.cos(ang), jnp.zeros((s, pad), F32)], axis=-1)
    sin = jnp.concatenate([jnp.zeros((s, MLA_NOPE), F32), jnp.sin(ang), jnp.zeros((s, pad), F32)], axis=-1)
    return cos, sin


def _lane_tile(nope, rope):
    lead = nope.shape[:-1]
    pad = jnp.zeros(lead + (LANE - MLA_NOPE - MLA_ROPE,), nope.dtype)
    return jnp.concatenate([nope, rope, pad], axis=-1)


def _prep_weights(p):
    partner, sign = _rope_perm()
    w_in = p["w_in"]
    depth, d, _ = w_in.shape
    cols = np.cumsum((NA_WIDTH, NA_WIDTH, NA_WIDTH, MLA_Q_RANK, MLA_KV_RANK, MLA_ROPE, LRU_WIDTH, LRU_WIDTH))
    w_kr = w_in[..., cols[4]:cols[5]]
    zn = jnp.zeros((depth, d, MLA_NOPE), F32)
    w_all = jnp.concatenate([
        w_in[..., :cols[4]], _lane_tile(zn, w_kr), _lane_tile(zn, w_kr[..., partner]), w_in[..., cols[5]:]], axis=-1)
    assert w_all.shape[-1] == W_ALL

    wq = p["mla_w_qb"].reshape(depth, MLA_Q_RANK, MLA_HEADS, MLA_NOPE + MLA_ROPE)
    wq_n, wq_r = wq[..., :MLA_NOPE], wq[..., MLA_NOPE:]
    w_qb = jnp.concatenate([
        _lane_tile(wq_n, wq_r).reshape(depth, MLA_Q_RANK, MLA_QK_W),
        _lane_tile(jnp.zeros_like(wq_n), wq_r[..., partner]).reshape(depth, MLA_Q_RANK, MLA_QK_W)], axis=-1)

    wkv = p["mla_w_kvb"].reshape(depth, MLA_KV_RANK, MLA_HEADS, MLA_NOPE + MLA_V)
    wk_n, wv = wkv[..., :MLA_NOPE], wkv[..., MLA_NOPE:]
    w_kvb = jnp.concatenate([
        _lane_tile(wk_n, jnp.zeros(wk_n.shape[:-1] + (MLA_ROPE,), F32)).reshape(depth, MLA_KV_RANK, MLA_QK_W),
        wv.reshape(depth, MLA_KV_RANK, MLA_HEADS * MLA_V)], axis=-1)

    qg, kg = p["mla_q_g"], p["mla_k_g"]
    z_n, z_r = jnp.zeros((depth, MLA_NOPE), F32), jnp.zeros((depth, MLA_ROPE), F32)
    sgn = jnp.asarray(sign)
    zrow = jnp.zeros((depth, LANE), F32)
    g_mla = jnp.stack([
        _lane_tile(qg[:, :MLA_NOPE], qg[:, MLA_NOPE:]) * MLA_SCALE,
        _lane_tile(z_n, sgn * qg[:, MLA_NOPE:][:, partner]) * MLA_SCALE,
        _lane_tile(kg[:, :MLA_NOPE], z_r),
        _lane_tile(z_n, kg[:, MLA_NOPE:]),
        _lane_tile(z_n, sgn * kg[:, MLA_NOPE:][:, partner]),
        zrow, zrow, zrow], axis=1)

    g_na = jnp.stack([jnp.tile(p["na_q_g"], (1, NA_HEADS)) * NA_SCALE, jnp.tile(p["na_k_g"], (1, NA_HEADS))], axis=1)

    wa, wx = p["lru_w_a"], p["lru_w_x"]
    wg4 = jnp.stack([wa[:, 0], wa[:, 1], wx[:, 0], wx[:, 1]], axis=1)
    eye = jnp.eye(LRU_HEADS, dtype=F32)
    w_gate = (wg4[:, :, :, :, None, :] * eye[None, None, :, None, :, None])
    w_gate = w_gate.transpose(0, 2, 3, 1, 4, 5).reshape(depth, LRU_WIDTH, 4 * LRU_WIDTH)
    ba, bx = p["lru_b_a"], p["lru_b_x"]
    b_gate = jnp.concatenate([ba[:, 0], ba[:, 1], bx[:, 0], bx[:, 1]], axis=-1)[:, None]

    return {
        "g_mix": p["norm_mix_g"][:, None], "g_ffn": p["norm_ffn_g"][:, None],
        "w_all": w_all.astype(BF), "w_qb": w_qb.astype(BF), "w_kvb": w_kvb.astype(BF),
        "g_na": g_na, "g_cq": p["mla_cq_g"][:, None], "g_ckv": p["mla_ckv_g"][:, None], "g_mla": g_mla,
        "conv_w": p["lru_conv_w"], "conv_b": p["lru_conv_b"][:, None],
        "w_gate": w_gate.astype(BF), "b_gate": b_gate, "lam": p["lru_lambda"].reshape(depth, 1, 2 * LRU_WIDTH),
        "w_out": p["w_out"].astype(BF), "ffn_gate": p["ffn_w_gate"].astype(BF),
        "ffn_up": p["ffn_w_up"].astype(BF), "ffn_down": p["ffn_w_down"].astype(BF),
    }


def _seg_mean_matrices():
    i = np.arange(LANE)
    e_na = (i[:, None] // NA_HEAD_DIM == i[None, :] // NA_HEAD_DIM).astype(np.float32) / NA_HEAD_DIM
    nope = i < MLA_NOPE
    rope = (i >= MLA_NOPE) & (i < MLA_NOPE + MLA_ROPE)
    e_mla = ((nope[:, None] & nope[None, :]).astype(np.float32) / MLA_NOPE
             + (rope[:, None] & rope[None, :]).astype(np.float32) / MLA_ROPE)
    return jnp.asarray(e_na, BF), jnp.asarray(e_mla, BF)


def kernel(x, c, ctx, c_ctx, ada_w, ada_b, norm_mix_g, norm_ffn_g, w_in, na_q_g, na_k_g, na_rpb, mla_cq_g, mla_w_qb, mla_ckv_g, mla_w_kvb, mla_q_g, mla_k_g, lru_conv_w, lru_conv_b, lru_w_a, lru_b_a, lru_w_x, lru_b_x, lru_lambda, w_out, ffn_w_gate, ffn_w_up, ffn_w_down):
    batch, s, d = x.shape
    cl = ctx.shape[1]
    depth = ada_w.shape[0]
    rows = s // GRID_W
    assert s % GRID_W == 0 and rows % NA_QROWS == 0 and rows >= 2 * NA_KROWS
    p = dict(norm_mix_g=norm_mix_g, norm_ffn_g=norm_ffn_g, w_in=w_in, na_q_g=na_q_g, na_k_g=na_k_g,
             mla_cq_g=mla_cq_g, mla_w_qb=mla_w_qb, mla_ckv_g=mla_ckv_g, mla_w_kvb=mla_w_kvb, mla_q_g=mla_q_g,
             mla_k_g=mla_k_g, lru_conv_w=lru_conv_w, lru_conv_b=lru_conv_b, lru_w_a=lru_w_a, lru_b_a=lru_b_a,
             lru_w_x=lru_w_x, lru_b_x=lru_b_x, lru_lambda=lru_lambda, w_out=w_out, ffn_w_gate=ffn_w_gate,
             ffn_w_up=ffn_w_up, ffn_w_down=ffn_w_down)

    tm = 512 if s % 512 == 0 else GRID_W * NA_QROWS
    tmc = tm if (batch * cl) % tm == 0 else cl
    tq = 256

    nmod = -(-(batch + 1) // SUBLANE) * SUBLANE
    cvec = jnp.concatenate([c, c_ctx[None], jnp.zeros((nmod - batch - 1, d), F32)], axis=0)
    mods = _ada_call(cvec, ada_w, ada_b).reshape(depth, nmod, 6, 1, d)
    tiles_per_batch = s // tm
    lat_row = lambda i: i // tiles_per_batch
    ctx_row = lambda i: batch

    wts = _prep_weights(p)
    wts["e_na"], wts["e_mla"] = _seg_mean_matrices()
    na_bias = _na_bias_tables(na_rpb, rows)
    cos, sin = _rope_tables(s)
    cos_c = jnp.concatenate([jnp.ones((tmc, MLA_NOPE + MLA_ROPE), F32),
                             jnp.zeros((tmc, LANE - MLA_NOPE - MLA_ROPE), F32)], axis=-1)
    sin_c = jnp.zeros((tmc, LANE), F32)

    h = x.reshape(batch * s, d)
    hc = ctx.reshape(batch * cl, d)
    for l in range(depth):
        last = l == depth - 1
        q, k, v, mq, mk, mv, lx, lg = _in_call(h, mods, lat_row, l, wts, cos, sin, tm)
        cq_, ck, cv, cmq, cmk, cmv, clx, clg = _in_call(hc, mods, ctx_row, l, wts, cos_c, sin_c, tmc)

        na_o = _na_call(q, k, v, ck, cv, na_bias, l, batch)
        mla_o = _mla_call(mq, mk, mv, cmk, cmv, batch, tq)
        lru_o, lru_c = _lru_call(lx, lg, clx, clg, l, wts, batch)
        h = _out_call(h, na_o, mla_o, lru_o, mods, lat_row, l, wts, tm)

        if not last:
            na_c, mla_c = _ctx_attn_call(cq_, ck, cv, cmq, cmk, cmv, batch)
            hc = _out_call(hc, na_c, mla_c, lru_c, mods, ctx_row, l, wts, tmc)
    return h.reshape(batch, s, d)
```

```python
import functools
import math

import numpy as np
import jax
import jax.numpy as jnp
from jax import lax
from jax.experimental import pallas as pl
from jax.experimental.pallas import tpu as pltpu

GRID_W = 64
NA_HEADS = 6
NA_HEAD_DIM = 64
NA_WIDTH = NA_HEADS * NA_HEAD_DIM
NA_WIN_H = 8
NA_WIN_W = 16
MLA_HEADS = 6
MLA_Q_RANK = 256
MLA_KV_RANK = 128
MLA_NOPE = 64
MLA_ROPE = 32
MLA_V = 64
LRU_WIDTH = 256
LRU_HEADS = 4
LRU_BLOCK = LRU_WIDTH // LRU_HEADS
LRU_CONV_W = 4
LRU_C = 8.0
ROPE_BASE = 10000.0
EPS = 1e-6
LOG2E = math.log2(math.e)
NA_SCALE = NA_HEAD_DIM ** -0.5 * LOG2E
MLA_SCALE = (MLA_NOPE + MLA_ROPE) ** -0.5 * LOG2E

LANE = 128
SUBLANE = 8
MLA_QK_W = MLA_HEADS * LANE
NEG_BIG = -1e30
VMEM_LIMIT = 56 * 1024 * 1024

NA_QROWS = 4
NA_KROWS = NA_QROWS + NA_WIN_H
NA_CLASSES = 3

C_Q, C_K, C_V = 0, NA_WIDTH, 2 * NA_WIDTH
C_CQ = 3 * NA_WIDTH
C_CKV = C_CQ + MLA_Q_RANK
C_KR = C_CKV + MLA_KV_RANK
C_KRP = C_KR + LANE
C_LX = C_KRP + LANE
C_LG = C_LX + LRU_WIDTH
W_ALL = C_LG + LRU_WIDTH

BF = jnp.bfloat16
F32 = jnp.float32


def _dot(a, b):
    return jnp.dot(a, b, preferred_element_type=F32)


def _dot_nt(a, b):
    return lax.dot_general(a, b, (((1,), (1,)), ((), ())), preferred_element_type=F32)


def _sigmoid(x):
    return 0.5 + 0.5 * jnp.tanh(0.5 * x)


def _silu(x):
    hx = 0.5 * x
    return hx + hx * jnp.tanh(hx)


def _gelu_tanh(x):
    return 0.5 * x * (1.0 + jnp.tanh(0.7978845608028654 * (x + 0.044715 * (x * x * x))))


def _params(*sem):
    return pltpu.CompilerParams(dimension_semantics=sem, vmem_limit_bytes=VMEM_LIMIT)


def _const_spec(shape):
    nd = len(shape)
    return pl.BlockSpec(shape, lambda *_: (0,) * nd, pipeline_mode=pl.Buffered(1))


def _layer_spec(arr, l):
    nd = arr.ndim
    return pl.BlockSpec((None,) + arr.shape[1:], lambda *_: (l,) + (0,) * (nd - 1), pipeline_mode=pl.Buffered(1))


def _mod_spec(mods, l, which, row_map):
    d = mods.shape[-1]
    return pl.BlockSpec((None, None, None, 1, d), lambda i: (l, row_map(i), which, 0, 0))


def _ada_kernel(c_ref, w_ref, b_ref, o_ref):
    s = _silu(c_ref[...]).astype(BF)
    o_ref[...] = _dot(s, w_ref[...].astype(BF)) + b_ref[...]


def _ada_call(cvec, ada_w, ada_b):
    depth, d, d6 = ada_w.shape
    r = cvec.shape[0]
    tn = 1024
    return pl.pallas_call(
        _ada_kernel,
        grid=(depth, d6 // tn),
        in_specs=[
            pl.BlockSpec((r, d), lambda l, j: (0, 0)),
            pl.BlockSpec((None, d, tn), lambda l, j: (l, 0, j)),
            pl.BlockSpec((None, 1, tn), lambda l, j: (l, 0, j)),
        ],
        out_specs=pl.BlockSpec((None, r, tn), lambda l, j: (l, 0, j)),
        out_shape=jax.ShapeDtypeStruct((depth, r, d6), F32),
        compiler_params=_params("parallel", "parallel"),
        name="adaln",
    )(cvec, ada_w, ada_b.reshape(depth, 1, d6))


def _in_kernel(h_ref, sh_ref, sc_ref, g_ref, wall_ref, wqb_ref, wkvb_ref, gna_ref, gcq_ref, gckv_ref,
               gm_ref, ena_ref, emla_ref, cos_ref, sin_ref,
               q_ref, k_ref, v_ref, mq_ref, mk_ref, mv_ref, lx_ref, lg_ref):
    x = h_ref[...]
    y = x * lax.rsqrt(jnp.mean(x * x, axis=-1, keepdims=True) + EPS) * g_ref[...]
    y = y * (1.0 + sc_ref[...]) + sh_ref[...]
    u = _dot(y.astype(BF), wall_ref[...])

    pair = 2 * LANE
    ena = ena_ref[...]
    qk = []
    for p in range(2 * NA_WIDTH // pair):
        x2 = u[:, C_Q + p * pair:C_Q + (p + 1) * pair]
        ms = _dot((x2 * x2).astype(BF), ena)
        qk.append((x2 * lax.rsqrt(ms + EPS) * gna_ref[:, p * pair:(p + 1) * pair]).astype(BF))
    qk = jnp.concatenate(qk, axis=-1)
    q_ref[...] = qk[:, :NA_WIDTH]
    k_ref[...] = qk[:, NA_WIDTH:]
    v_ref[...] = u[:, C_V:C_V + NA_WIDTH].astype(BF)

    emla = emla_ref[...]
    cos = cos_ref[...]
    sin = sin_ref[...]
    cos2 = jnp.concatenate([cos, cos], axis=-1)
    sin2 = jnp.concatenate([sin, sin], axis=-1)
    cq = u[:, C_CQ:C_CQ + MLA_Q_RANK]
    cqn = cq * lax.rsqrt(jnp.mean(cq * cq, axis=-1, keepdims=True) + EPS) * gcq_ref[...]
    qq = _dot(cqn.astype(BF), wqb_ref[...])
    a_q = gm_ref[0:1, :] * cos2
    b_q = gm_ref[1:2, :] * sin2
    for p in range(MLA_QK_W // pair):
        lo = p * pair
        qr = qq[:, lo:lo + pair]
        qp = qq[:, MLA_QK_W + lo:MLA_QK_W + lo + pair]
        ms = _dot((qr * qr).astype(BF), emla)
        mq_ref[:, lo:lo + pair] = (lax.rsqrt(ms + EPS) * (qr * a_q + qp * b_q)).astype(BF)

    ckv = u[:, C_CKV:C_CKV + MLA_KV_RANK]
    ckvn = ckv * lax.rsqrt(jnp.mean(ckv * ckv, axis=-1, keepdims=True) + EPS) * gckv_ref[...]
    kv = _dot(ckvn.astype(BF), wkvb_ref[...])
    krr = u[:, C_KR:C_KR + LANE]
    krp = u[:, C_KRP:C_KRP + LANE]
    msr = _dot((krr * krr).astype(BF), emla_ref[0:LANE, 0:LANE])
    kr = lax.rsqrt(msr + EPS) * (krr * (gm_ref[3:4, 0:LANE] * cos) + krp * (gm_ref[4:5, 0:LANE] * sin))
    kr2 = jnp.concatenate([kr, kr], axis=-1)
    for p in range(MLA_QK_W // pair):
        lo = p * pair
        kn = kv[:, lo:lo + pair]
        ms = _dot((kn * kn).astype(BF), emla)
        mk_ref[:, lo:lo + pair] = (kn * lax.rsqrt(ms + EPS) * gm_ref[2:3, :] + kr2).astype(BF)
    mv_ref[...] = kv[:, MLA_QK_W:].astype(BF)

    lx_ref[...] = u[:, C_LX:C_LX + LRU_WIDTH]
    lg_ref[...] = u[:, C_LG:C_LG + LRU_WIDTH]


def _in_call(h, mods, row_map, l, wts, cos, sin, tm):
    n, d = h.shape
    tab_tiles = cos.shape[0] // tm
    row = lambda w: pl.BlockSpec((tm, w), lambda i: (i, 0))
    tab = pl.BlockSpec((tm, LANE), lambda i: (i % tab_tiles, 0))
    stacked = [wts["g_mix"], wts["w_all"], wts["w_qb"], wts["w_kvb"], wts["g_na"], wts["g_cq"], wts["g_ckv"],
               wts["g_mla"]]
    consts = [wts["e_na"], wts["e_mla"]]
    out_w = [(NA_WIDTH, BF), (NA_WIDTH, BF), (NA_WIDTH, BF), (MLA_QK_W, BF), (MLA_QK_W, BF),
             (MLA_HEADS * MLA_V, BF), (LRU_WIDTH, F32), (LRU_WIDTH, F32)]
    return pl.pallas_call(
        _in_kernel,
        grid=(n // tm,),
        in_specs=[row(d), _mod_spec(mods, l, 0, row_map), _mod_spec(mods, l, 1, row_map)]
                 + [_layer_spec(a, l) for a in stacked] + [_const_spec(a.shape) for a in consts] + [tab, tab],
        out_specs=[row(w) for w, _ in out_w],
        out_shape=[jax.ShapeDtypeStruct((n, w), dt) for w, dt in out_w],
        compiler_params=_params("parallel"),
        name="in_proj",
    )(h, mods, mods, *stacked, *consts, cos, sin)


def _softmax_pv(s_a, s_b, v_a, v_b):
    m = jnp.maximum(jnp.max(s_a, axis=-1, keepdims=True), jnp.max(s_b, axis=-1, keepdims=True))
    p_a = jnp.exp2(s_a - m)
    p_b = jnp.exp2(s_b - m)
    l = jnp.sum(p_a, axis=-1, keepdims=True) + jnp.sum(p_b, axis=-1, keepdims=True)
    o = _dot(p_a.astype(BF), v_a) + _dot(p_b.astype(BF), v_b)
    return o / l


def _na_block_geometry(rows):
    nblk = rows // NA_QROWS
    half = NA_WIN_H // 2
    ks = np.clip(np.arange(nblk) * NA_QROWS - half, 0, rows - NA_KROWS)
    pats = []
    for j in range(nblk):
        r = j * NA_QROWS + np.arange(NA_QROWS)
        rs = np.clip(r - half, 0, rows - NA_WIN_H)
        assert ks[j] <= rs.min() and rs.max() + NA_WIN_H <= ks[j] + NA_KROWS
        pats.append(tuple(zip((r - ks[j]).tolist(), (rs - ks[j]).tolist())))
    cls = [0 if j == 0 else (2 if j == nblk - 1 else 1) for j in range(nblk)]
    by_cls = {}
    for j in range(nblk):
        assert by_cls.setdefault(cls[j], pats[j]) == pats[j]
    return [by_cls[c] for c in range(NA_CLASSES)]


def _na_kernel(q_ref, k_ref, v_ref, kc_ref, vc_ref, bias_ref, o_ref, *, rows):
    j = pl.program_id(1)
    nblk = rows // NA_QROWS
    ks = jnp.clip(j * NA_QROWS - NA_WIN_H // 2, 0, rows - NA_KROWS)
    start = pl.multiple_of(ks * GRID_W, GRID_W)
    cls = jnp.where(j == 0, 0, jnp.where(j == nblk - 1, 2, 1))
    nq = NA_QROWS * GRID_W
    nk = NA_KROWS * GRID_W
    first_half = lax.broadcasted_iota(jnp.int32, (nq, LANE), 1) < NA_HEAD_DIM
    def scores(hd):
        lo = (hd // 2) * LANE
        q2 = q_ref[:, lo:lo + LANE]
        qm = jnp.where(first_half if hd % 2 == 0 else ~first_half, q2, jnp.zeros_like(q2))
        s_w = _dot_nt(qm, k_ref[pl.ds(start, nk), lo:lo + LANE]) + bias_ref[cls, hd]
        return s_w, _dot_nt(qm, kc_ref[:, lo:lo + LANE])

    nxt = scores(0)
    outs = []
    for hd in range(NA_HEADS):
        s_w, s_c = nxt
        if hd + 1 < NA_HEADS:
            nxt = scores(hd + 1)
        lo = (hd // 2) * LANE
        outs.append(_softmax_pv(s_w, s_c, v_ref[pl.ds(start, nk), lo:lo + LANE], vc_ref[:, lo:lo + LANE]))
        if hd % 2 == 1:
            o_ref[:, lo:lo + LANE] = jnp.where(first_half, outs[0], outs[1]).astype(BF)
            outs = []


def _na_call(q, k, v, kc, vc, bias, l, batch):
    n, w = q.shape
    s = n // batch
    c = kc.shape[0] // batch
    rows = s // GRID_W
    q3, k3, v3 = (a.reshape(batch, s, w) for a in (q, k, v))
    kc3, vc3 = (a.reshape(batch, c, w) for a in (kc, vc))
    nq = NA_QROWS * GRID_W
    out = pl.pallas_call(
        functools.partial(_na_kernel, rows=rows),
        grid=(batch, rows // NA_QROWS),
        in_specs=[
            pl.BlockSpec((None, nq, w), lambda b, j: (b, j, 0)),
            pl.BlockSpec((None, s, w), lambda b, j: (b, 0, 0)),
            pl.BlockSpec((None, s, w), lambda b, j: (b, 0, 0)),
            pl.BlockSpec((None, c, w), lambda b, j: (b, 0, 0)),
            pl.BlockSpec((None, c, w), lambda b, j: (b, 0, 0)),
            _layer_spec(bias, l),
        ],
        out_specs=pl.BlockSpec((None, nq, w), lambda b, j: (b, j, 0)),
        out_shape=jax.ShapeDtypeStruct((batch, s, w), BF),
        compiler_params=_params("parallel", "arbitrary"),
        name="nbr_attn",
    )(q3, k3, v3, kc3, vc3, bias)
    return out.reshape(n, w)


def _na_bias_tables(rpb, rows):
    depth = rpb.shape[0]
    qc = np.arange(GRID_W)
    kcol = np.arange(GRID_W)
    cs = np.clip(qc - NA_WIN_W // 2, 0, GRID_W - NA_WIN_W)
    ok = (kcol[None, :] >= cs[:, None]) & (kcol[None, :] < cs[:, None] + NA_WIN_W)
    col_rel = np.clip(kcol[None, :] - qc[:, None], 1 - NA_WIN_W, NA_WIN_W - 1) + NA_WIN_W - 1
    sel = (col_rel[None] == np.arange(2 * NA_WIN_W - 1)[:, None, None]).astype(np.float32)
    t = jnp.einsum("lhrd,dqk->lhrqk", rpb.astype(F32), jnp.asarray(sel), precision=lax.Precision.HIGHEST)
    t = jnp.where(ok, t * LOG2E, NEG_BIG)
    neg = jnp.full((depth, NA_HEADS, GRID_W, GRID_W), NEG_BIG, F32)
    classes = []
    for pat in _na_block_geometry(rows):
        qrows = []
        for qo, wo in pat:
            blocks = [t[:, :, kr - qo + NA_WIN_H - 1] if wo <= kr < wo + NA_WIN_H else neg for kr in range(NA_KROWS)]
            qrows.append(jnp.concatenate(blocks, axis=-1))
        classes.append(jnp.concatenate(qrows, axis=-2))
    return jnp.stack(classes, axis=1)


def _mla_kernel(q_ref, k_ref, v_ref, kc_ref, vc_ref, o_ref):
    tq = q_ref.shape[0]
    first_half = lax.broadcasted_iota(jnp.int32, (tq, LANE), 1) < MLA_V

    def scores(hd):
        hl = hd * LANE
        qh = q_ref[:, hl:hl + LANE]
        return _dot_nt(qh, k_ref[:, hl:hl + LANE]), _dot_nt(qh, kc_ref[:, hl:hl + LANE])

    nxt = scores(0)
    outs = []
    for hd in range(MLA_HEADS):
        s_l, s_c = nxt
        if hd + 1 < MLA_HEADS:
            nxt = scores(hd + 1)
        lo = (hd // 2) * LANE
        outs.append(_softmax_pv(s_l, s_c, v_ref[:, lo:lo + LANE], vc_ref[:, lo:lo + LANE]))
        if hd % 2 == 1:
            o_ref[:, lo:lo + LANE] = jnp.where(first_half, outs[0], outs[1]).astype(BF)
            outs = []


def _mla_call(q, k, v, kc, vc, batch, tq):
    n, wq = q.shape
    wv = v.shape[1]
    s = n // batch
    c = kc.shape[0] // batch
    q3, k3 = q.reshape(batch, s, wq), k.reshape(batch, s, wq)
    v3 = v.reshape(batch, s, wv)
    kc3, vc3 = kc.reshape(batch, c, wq), vc.reshape(batch, c, wv)
    out = pl.pallas_call(
        _mla_kernel,
        grid=(batch, s // tq),
        in_specs=[
            pl.BlockSpec((None, tq, wq), lambda b, i: (b, i, 0)),
            pl.BlockSpec((None, s, wq), lambda b, i: (b, 0, 0)),
            pl.BlockSpec((None, s, wv), lambda b, i: (b, 0, 0)),
            pl.BlockSpec((None, c, wq), lambda b, i: (b, 0, 0)),
            pl.BlockSpec((None, c, wv), lambda b, i: (b, 0, 0)),
        ],
        out_specs=pl.BlockSpec((None, tq, wv), lambda b, i: (b, i, 0)),
        out_shape=jax.ShapeDtypeStruct((batch, s, wv), BF),
        compiler_params=_params("parallel", "arbitrary"),
        name="latent_attn",
    )(q3, k3, v3, kc3, vc3)
    return out.reshape(n, wv)


def _softmax_pv1(s, v):
    m = jnp.max(s, axis=-1, keepdims=True)
    p = jnp.exp2(s - m)
    l = jnp.sum(p, axis=-1, keepdims=True)
    return _dot(p.astype(BF), v) / l


def _ctx_attn_kernel(q_ref, k_ref, v_ref, mq_ref, mk_ref, mv_ref, ona_ref, omla_ref):
    c = q_ref.shape[0]
    first_half = lax.broadcasted_iota(jnp.int32, (c, LANE), 1) < NA_HEAD_DIM
    for p in range(NA_WIDTH // LANE):
        lo = p * LANE
        q2 = q_ref[:, lo:lo + LANE]
        k2 = k_ref[:, lo:lo + LANE]
        v2 = v_ref[:, lo:lo + LANE]
        outs = []
        for hh in range(2):
            qm = jnp.where(first_half if hh == 0 else ~first_half, q2, jnp.zeros_like(q2))
            outs.append(_softmax_pv1(_dot_nt(qm, k2), v2))
        ona_ref[:, lo:lo + LANE] = jnp.where(first_half, outs[0], outs[1]).astype(BF)
    for p in range(MLA_HEADS // 2):
        lo = p * LANE
        v2 = mv_ref[:, lo:lo + LANE]
        outs = []
        for hh in range(2):
            hl = (2 * p + hh) * LANE
            outs.append(_softmax_pv1(_dot_nt(mq_ref[:, hl:hl + LANE], mk_ref[:, hl:hl + LANE]), v2))
        omla_ref[:, lo:lo + LANE] = jnp.where(first_half, outs[0], outs[1]).astype(BF)


def _ctx_attn_call(q, k, v, mq, mk, mv, batch):
    nc = q.shape[0]
    c = nc // batch
    ins = [q, k, v, mq, mk, mv]
    ins3 = [a.reshape(batch, c, a.shape[1]) for a in ins]
    spec = lambda w: pl.BlockSpec((None, c, w), lambda b: (b, 0, 0))
    ona, omla = pl.pallas_call(
        _ctx_attn_kernel,
        grid=(batch,),
        in_specs=[spec(a.shape[1]) for a in ins],
        out_specs=[spec(NA_WIDTH), spec(MLA_HEADS * MLA_V)],
        out_shape=[jax.ShapeDtypeStruct((batch, c, NA_WIDTH), BF),
                   jax.ShapeDtypeStruct((batch, c, MLA_HEADS * MLA_V), BF)],
        compiler_params=_params("parallel"),
        name="ctx_attn",
    )(*ins3)
    return ona.reshape(nc, NA_WIDTH), omla.reshape(nc, MLA_HEADS * MLA_V)


LRU_PAD = SUBLANE
LRU_CHUNK = 512


def _lru_coeffs(x_ref, n, xpad_ref, a_ref, u_ref, cw_ref, cb_ref, wg_ref, bg_ref, sp):
    w = LRU_WIDTH
    zeros = jnp.zeros((LRU_PAD, w), F32)
    xpad_ref[0:LRU_PAD, :] = zeros
    xpad_ref[LRU_PAD:LRU_PAD + n, :] = x_ref[...]
    xpad_ref[LRU_PAD + n:2 * LRU_PAD + n, :] = zeros
    left = (LRU_CONV_W - 1) // 2
    for c0 in range(0, n, LRU_CHUNK):
        cn = min(LRU_CHUNK, n - c0)
        xc = jnp.zeros((cn, w), F32) + cb_ref[...]
        for j in range(LRU_CONV_W):
            off = LRU_PAD + c0 + j - left
            xc = xc + cw_ref[j:j + 1, :] * xpad_ref[off:off + cn, :]
        g = _dot(xc.astype(BF), wg_ref[...]) + bg_ref[...]
        for dr in range(2):
            log_a = (-LRU_C) * _sigmoid(g[:, dr * w:(dr + 1) * w]) * sp[:, dr * w:(dr + 1) * w]
            a = jnp.exp(log_a)
            gx = _sigmoid(g[:, (2 + dr) * w:(3 + dr) * w])
            a_ref[dr, c0:c0 + cn, :] = a
            u_ref[dr, c0:c0 + cn, :] = jnp.sqrt(1.0 - a * a) * (gx * xc)


def _lru_scan(n, a_ref, u_ref, hf0, hb0):
    w = LRU_WIDTH
    nchunk = n // SUBLANE
    rowi = lax.broadcasted_iota(jnp.int32, (SUBLANE, w), 0)

    def body(c, carry):
        hf, hb = carry
        rf = pl.multiple_of(c * SUBLANE, SUBLANE)
        rb = pl.multiple_of((nchunk - 1 - c) * SUBLANE, SUBLANE)
        af = a_ref[0, pl.ds(rf, SUBLANE), :]
        uf = u_ref[0, pl.ds(rf, SUBLANE), :]
        ab = a_ref[1, pl.ds(rb, SUBLANE), :]
        ub = u_ref[1, pl.ds(rb, SUBLANE), :]
        for s in (1, 2, 4):
            mf = rowi >= s
            uf = jnp.where(mf, af * pltpu.roll(uf, s, 0) + uf, uf)
            af = jnp.where(mf, af * pltpu.roll(af, s, 0), af)
            mb = rowi < SUBLANE - s
            ub = jnp.where(mb, ab * pltpu.roll(ub, SUBLANE - s, 0) + ub, ub)
            ab = jnp.where(mb, ab * pltpu.roll(ab, SUBLANE - s, 0), ab)
        hfc = af * hf + uf
        hbc = ab * hb + ub
        u_ref[0, pl.ds(rf, SUBLANE), :] = hfc
        u_ref[1, pl.ds(rb, SUBLANE), :] = hbc
        return hfc[SUBLANE - 1:SUBLANE, :], hbc[0:1, :]

    return lax.fori_loop(0, nchunk, body, (hf0, hb0))


def _lru_kernel(lx_ref, lg_ref, clx_ref, clg_ref, cw_ref, cb_ref, wg_ref, bg_ref, lam_ref,
                o_ref, oc_ref, xpad_ref, a_ref, u_ref, ac_ref, uc_ref):
    n = lx_ref.shape[0]
    nc = clx_ref.shape[0]
    nlam = -lam_ref[...]
    sp = jnp.maximum(nlam, 0.0) + jnp.log(1.0 + jnp.exp(-jnp.abs(nlam)))
    zero = jnp.zeros((1, LRU_WIDTH), F32)
    _lru_coeffs(clx_ref, nc, xpad_ref, ac_ref, uc_ref, cw_ref, cb_ref, wg_ref, bg_ref, sp)
    hf0, hb0 = _lru_scan(nc, ac_ref, uc_ref, zero, zero)
    oc_ref[...] = (_gelu_tanh(clg_ref[...]) * (uc_ref[0] + uc_ref[1])).astype(BF)
    _lru_coeffs(lx_ref, n, xpad_ref, a_ref, u_ref, cw_ref, cb_ref, wg_ref, bg_ref, sp)
    _lru_scan(n, a_ref, u_ref, hf0, hb0)
    for c0 in range(0, n, LRU_CHUNK):
        cn = min(LRU_CHUNK, n - c0)
        hsum = u_ref[0, c0:c0 + cn, :] + u_ref[1, c0:c0 + cn, :]
        o_ref[c0:c0 + cn, :] = (_gelu_tanh(lg_ref[c0:c0 + cn, :]) * hsum).astype(BF)


def _lru_call(lx, lg, clx, clg, l, wts, batch):
    n, w = lx.shape
    s = n // batch
    c = clx.shape[0] // batch
    seq = lambda a, ln: a.reshape(batch, ln, w)
    lat = pl.BlockSpec((None, s, w), lambda b: (b, 0, 0))
    ctx = pl.BlockSpec((None, c, w), lambda b: (b, 0, 0))
    stacked = [wts["conv_w"], wts["conv_b"], wts["w_gate"], wts["b_gate"], wts["lam"]]
    o, oc = pl.pallas_call(
        _lru_kernel,
        grid=(batch,),
        in_specs=[lat, lat, ctx, ctx] + [_layer_spec(a, l) for a in stacked],
        out_specs=[lat, ctx],
        out_shape=[jax.ShapeDtypeStruct((batch, s, w), BF), jax.ShapeDtypeStruct((batch, c, w), BF)],
        scratch_shapes=[
            pltpu.VMEM((s + 2 * LRU_PAD, w), F32),
            pltpu.VMEM((2, s, w), F32),
            pltpu.VMEM((2, s, w), F32),
            pltpu.VMEM((2, c, w), F32),
            pltpu.VMEM((2, c, w), F32),
        ],
        compiler_params=_params("parallel"),
        name="rglru",
    )(seq(lx, s), seq(lg, s), seq(clx, c), seq(clg, c), *stacked)
    return o.reshape(n, w), oc.reshape(batch * c, w)


def _out_kernel(h_ref, na_ref, mla_ref, lru_ref, gm_ref, shf_ref, scf_ref, gf_ref, g_ref,
                wo_ref, wg_ref, wu_ref, wd_ref, o_ref, *, hid_chunk):
    o_cat = jnp.concatenate([na_ref[...], mla_ref[...], lru_ref[...]], axis=-1)
    h1 = h_ref[...] + gm_ref[...] * _dot(o_cat, wo_ref[...])
    y = h1 * lax.rsqrt(jnp.mean(h1 * h1, axis=-1, keepdims=True) + EPS) * g_ref[...]
    y = (y * (1.0 + scf_ref[...]) + shf_ref[...]).astype(BF)
    hidden = wg_ref.shape[1]

    def gate_up(c0):
        return _dot(y, wg_ref[:, c0:c0 + hid_chunk]), _dot(y, wu_ref[:, c0:c0 + hid_chunk])

    acc = None
    nxt = gate_up(0)
    for c0 in range(0, hidden, hid_chunk):
        gate, up = nxt
        if c0 + hid_chunk < hidden:
            nxt = gate_up(c0 + hid_chunk)
        d = _dot((_silu(gate) * up).astype(BF), wd_ref[c0:c0 + hid_chunk, :])
        acc = d if acc is None else acc + d
    o_ref[...] = h1 + gf_ref[...] * acc


def _out_call(h, na_o, mla_o, lru_o, mods, row_map, l, wts, tm):
    n, d = h.shape
    row = lambda w: pl.BlockSpec((tm, w), lambda i: (i, 0))
    stacked = [wts["g_ffn"], wts["w_out"], wts["ffn_gate"], wts["ffn_up"], wts["ffn_down"]]
    hidden = wts["ffn_gate"].shape[-1]
    hid_chunk = 2 * LANE if hidden % (2 * LANE) == 0 else hidden
    return pl.pallas_call(
        functools.partial(_out_kernel, hid_chunk=hid_chunk),
        grid=(n // tm,),
        in_specs=[row(d), row(na_o.shape[1]), row(mla_o.shape[1]), row(lru_o.shape[1])]
                 + [_mod_spec(mods, l, which, row_map) for which in (2, 3, 4, 5)]
                 + [_layer_spec(a, l) for a in stacked],
        out_specs=row(d),
        out_shape=jax.ShapeDtypeStruct((n, d), F32),
        compiler_params=_params("parallel"),
        name="out_ffn",
    )(h, na_o, mla_o, lru_o, mods, mods, mods, mods, *stacked)


def _rope_perm():
    j = np.arange(MLA_ROPE)
    first = (j % (MLA_ROPE // 2)) < MLA_ROPE // 4
    partner = np.where(first, j + MLA_ROPE // 4, j - MLA_ROPE // 4)
    sign = np.where(first, -1.0, 1.0).astype(np.float32)
    return partner, sign


def _rope_tables(s):
    t = np.arange(s)
    row = (t // GRID_W).astype(np.float32)
    col = (t % GRID_W).astype(np.float32)
    n_freq = MLA_ROPE // 4
    inv = jnp.asarray(ROPE_BASE, F32) ** (-jnp.arange(n_freq, dtype=F32) / n_freq)
    ar = jnp.asarray(row)[:, None] * inv
    ac = jnp.asarray(col)[:, None] * inv
    ang = jnp.concatenate([ar, ar, ac, ac], axis=-1)
    pad = LANE - MLA_NOPE - MLA_ROPE
    cos = jnp.concatenate([jnp.ones((s, MLA_NOPE), F32), jnp.cos(ang), jnp.zeros((s, pad), F32)], axis=-1)
    sin = jnp.concatenate([jnp.zeros((s, MLA_NOPE), F32), jnp.sin(ang), jnp.zeros((s, pad), F32)], axis=-1)
    return cos, sin


def _lane_tile(nope, rope):
    lead = nope.shape[:-1]
    pad = jnp.zeros(lead + (LANE - MLA_NOPE - MLA_ROPE,), nope.dtype)
    return jnp.concatenate([nope, rope, pad], axis=-1)


def _prep_weights(p):
    partner, sign = _rope_perm()
    w_in = p["w_in"]
    depth, d, _ = w_in.shape
    cols = np.cumsum((NA_WIDTH, NA_WIDTH, NA_WIDTH, MLA_Q_RANK, MLA_KV_RANK, MLA_ROPE, LRU_WIDTH, LRU_WIDTH))
    w_kr = w_in[..., cols[4]:cols[5]]
    zn = jnp.zeros((depth, d, MLA_NOPE), F32)
    w_all = jnp.concatenate([
        w_in[..., :cols[4]], _lane_tile(zn, w_kr), _lane_tile(zn, w_kr[..., partner]), w_in[..., cols[5]:]], axis=-1)
    assert w_all.shape[-1] == W_ALL

    wq = p["mla_w_qb"].reshape(depth, MLA_Q_RANK, MLA_HEADS, MLA_NOPE + MLA_ROPE)
    wq_n, wq_r = wq[..., :MLA_NOPE], wq[..., MLA_NOPE:]
    w_qb = jnp.concatenate([
        _lane_tile(wq_n, wq_r).reshape(depth, MLA_Q_RANK, MLA_QK_W),
        _lane_tile(jnp.zeros_like(wq_n), wq_r[..., partner]).reshape(depth, MLA_Q_RANK, MLA_QK_W)], axis=-1)

    wkv = p["mla_w_kvb"].reshape(depth, MLA_KV_RANK, MLA_HEADS, MLA_NOPE + MLA_V)
    wk_n, wv = wkv[..., :MLA_NOPE], wkv[..., MLA_NOPE:]
    w_kvb = jnp.concatenate([
        _lane_tile(wk_n, jnp.zeros(wk_n.shape[:-1] + (MLA_ROPE,), F32)).reshape(depth, MLA_KV_RANK, MLA_QK_W),
        wv.reshape(depth, MLA_KV_RANK, MLA_HEADS * MLA_V)], axis=-1)

    qg, kg = p["mla_q_g"], p["mla_k_g"]
    z_n, z_r = jnp.zeros((depth, MLA_NOPE), F32), jnp.zeros((depth, MLA_ROPE), F32)
    sgn = jnp.asarray(sign)
    zrow = jnp.zeros((depth, LANE), F32)
    g_mla = jnp.stack([
        _lane_tile(qg[:, :MLA_NOPE], qg[:, MLA_NOPE:]) * MLA_SCALE,
        _lane_tile(z_n, sgn * qg[:, MLA_NOPE:][:, partner]) * MLA_SCALE,
        _lane_tile(kg[:, :MLA_NOPE], z_r),
        _lane_tile(z_n, kg[:, MLA_NOPE:]),
        _lane_tile(z_n, sgn * kg[:, MLA_NOPE:][:, partner]),
        zrow, zrow, zrow], axis=1)
    g_mla = jnp.concatenate([g_mla, g_mla], axis=-1)

    g_na = jnp.concatenate([jnp.tile(p["na_q_g"], (1, NA_HEADS)) * NA_SCALE, jnp.tile(p["na_k_g"], (1, NA_HEADS))],
                           axis=-1)[:, None]

    wa, wx = p["lru_w_a"], p["lru_w_x"]
    wg4 = jnp.stack([wa[:, 0], wa[:, 1], wx[:, 0], wx[:, 1]], axis=1)
    eye = jnp.eye(LRU_HEADS, dtype=F32)
    w_gate = (wg4[:, :, :, :, None, :] * eye[None, None, :, None, :, None])
    w_gate = w_gate.transpose(0, 2, 3, 1, 4, 5).reshape(depth, LRU_WIDTH, 4 * LRU_WIDTH)
    ba, bx = p["lru_b_a"], p["lru_b_x"]
    b_gate = jnp.concatenate([ba[:, 0], ba[:, 1], bx[:, 0], bx[:, 1]], axis=-1)[:, None]

    return {
        "g_mix": p["norm_mix_g"][:, None], "g_ffn": p["norm_ffn_g"][:, None],
        "w_all": w_all.astype(BF), "w_qb": w_qb.astype(BF), "w_kvb": w_kvb.astype(BF),
        "g_na": g_na, "g_cq": p["mla_cq_g"][:, None], "g_ckv": p["mla_ckv_g"][:, None], "g_mla": g_mla,
        "conv_w": p["lru_conv_w"], "conv_b": p["lru_conv_b"][:, None],
        "w_gate": w_gate.astype(BF), "b_gate": b_gate, "lam": p["lru_lambda"].reshape(depth, 1, 2 * LRU_WIDTH),
        "w_out": p["w_out"].astype(BF), "ffn_gate": p["ffn_w_gate"].astype(BF),
        "ffn_up": p["ffn_w_up"].astype(BF), "ffn_down": p["ffn_w_down"].astype(BF),
    }


def _seg_mean_matrices():
    i = np.arange(2 * LANE)
    e_na = (i[:, None] // NA_HEAD_DIM == i[None, :] // NA_HEAD_DIM).astype(np.float32) / NA_HEAD_DIM
    tile = i // LANE
    nope = (i % LANE) < MLA_NOPE
    rope = ((i % LANE) >= MLA_NOPE) & ((i % LANE) < MLA_NOPE + MLA_ROPE)
    same = tile[:, None] == tile[None, :]
    e_mla = (same & nope[:, None] & nope[None, :]).astype(np.float32) / MLA_NOPE \
        + (same & rope[:, None] & rope[None, :]).astype(np.float32) / MLA_ROPE
    return jnp.asarray(e_na, BF), jnp.asarray(e_mla, BF)


def kernel(x, c, ctx, c_ctx, ada_w, ada_b, norm_mix_g, norm_ffn_g, w_in, na_q_g, na_k_g, na_rpb, mla_cq_g, mla_w_qb, mla_ckv_g, mla_w_kvb, mla_q_g, mla_k_g, lru_conv_w, lru_conv_b, lru_w_a, lru_b_a, lru_w_x, lru_b_x, lru_lambda, w_out, ffn_w_gate, ffn_w_up, ffn_w_down):
    batch, s, d = x.shape
    cl = ctx.shape[1]
    depth = ada_w.shape[0]
    rows = s // GRID_W
    assert s % GRID_W == 0 and rows % NA_QROWS == 0 and rows >= 2 * NA_KROWS
    p = dict(norm_mix_g=norm_mix_g, norm_ffn_g=norm_ffn_g, w_in=w_in, na_q_g=na_q_g, na_k_g=na_k_g,
             mla_cq_g=mla_cq_g, mla_w_qb=mla_w_qb, mla_ckv_g=mla_ckv_g, mla_w_kvb=mla_w_kvb, mla_q_g=mla_q_g,
             mla_k_g=mla_k_g, lru_conv_w=lru_conv_w, lru_conv_b=lru_conv_b, lru_w_a=lru_w_a, lru_b_a=lru_b_a,
             lru_w_x=lru_w_x, lru_b_x=lru_b_x, lru_lambda=lru_lambda, w_out=w_out, ffn_w_gate=ffn_w_gate,
             ffn_w_up=ffn_w_up, ffn_w_down=ffn_w_down)

    tm = 512 if s % 512 == 0 else GRID_W * NA_QROWS
    tmc = tm if (batch * cl) % tm == 0 else cl
    tq = 256

    nmod = -(-(batch + 1) // SUBLANE) * SUBLANE
    cvec = jnp.concatenate([c, c_ctx[None], jnp.zeros((nmod - batch - 1, d), F32)], axis=0)
    mods = _ada_call(cvec, ada_w, ada_b).reshape(depth, nmod, 6, 1, d)
    tiles_per_batch = s // tm
    lat_row = lambda i: i // tiles_per_batch
    ctx_row = lambda i: batch

    wts = _prep_weights(p)
    wts["e_na"], wts["e_mla"] = _seg_mean_matrices()
    na_bias = _na_bias_tables(na_rpb, rows)
    cos, sin = _rope_tables(s)
    cos_c = jnp.concatenate([jnp.ones((tmc, MLA_NOPE + MLA_ROPE), F32),
                             jnp.zeros((tmc, LANE - MLA_NOPE - MLA_ROPE), F32)], axis=-1)
    sin_c = jnp.zeros((tmc, LANE), F32)

    h = x.reshape(batch * s, d)
    hc = ctx.reshape(batch * cl, d)
    for l in range(depth):
        last = l == depth - 1
        q, k, v, mq, mk, mv, lx, lg = _in_call(h, mods, lat_row, l, wts, cos, sin, tm)
        cq_, ck, cv, cmq, cmk, cmv, clx, clg = _in_call(hc, mods, ctx_row, l, wts, cos_c, sin_c, tmc)

        na_o = _na_call(q, k, v, ck, cv, na_bias, l, batch)
        mla_o = _mla_call(mq, mk, mv, cmk, cmv, batch, tq)
        lru_o, lru_c = _lru_call(lx, lg, clx, clg, l, wts, batch)
        h = _out_call(h, na_o, mla_o, lru_o, mods, lat_row, l, wts, tm)

        if not last:
            na_c, mla_c = _ctx_attn_call(cq_, ck, cv, cmq, cmk, cmv, batch)
            hc = _out_call(hc, na_c, mla_c, lru_c, mods, ctx_row, l, wts, tmc)
    return h.reshape(batch, s, d)
```

```python
import functools
import math

import numpy as np
import jax
import jax.numpy as jnp
from jax import lax
from jax.experimental import pallas as pl
from jax.experimental.pallas import tpu as pltpu

GRID_W = 64
NA_HEADS = 6
NA_HEAD_DIM = 64
NA_WIDTH = NA_HEADS * NA_HEAD_DIM
NA_WIN_H = 8
NA_WIN_W = 16
MLA_HEADS = 6
MLA_Q_RANK = 256
MLA_KV_RANK = 128
MLA_NOPE = 64
MLA_ROPE = 32
MLA_V = 64
LRU_WIDTH = 256
LRU_HEADS = 4
LRU_BLOCK = LRU_WIDTH // LRU_HEADS
LRU_CONV_W = 4
LRU_C = 8.0
ROPE_BASE = 10000.0
EPS = 1e-6
LOG2E = math.log2(math.e)
NA_SCALE = NA_HEAD_DIM ** -0.5 * LOG2E
MLA_SCALE = (MLA_NOPE + MLA_ROPE) ** -0.5 * LOG2E

LANE = 128
SUBLANE = 8
MLA_QK_W = MLA_HEADS * LANE
NEG_BIG = -1e30
VMEM_LIMIT = 56 * 1024 * 1024

NA_QROWS = 4
NA_KROWS = NA_QROWS + NA_WIN_H
NA_CLASSES = 3
NA_BLOCKS_PER_STEP = 4

C_Q, C_K, C_V = 0, NA_WIDTH, 2 * NA_WIDTH
C_CQ = 3 * NA_WIDTH
C_CKV = C_CQ + MLA_Q_RANK
C_KR = C_CKV + MLA_KV_RANK
C_KRP = C_KR + LANE
C_LX = C_KRP + LANE
C_LG = C_LX + LRU_WIDTH
W_ALL = C_LG + LRU_WIDTH

BF = jnp.bfloat16
F32 = jnp.float32


def _dot(a, b):
    return jnp.dot(a, b, preferred_element_type=F32)


def _dot_nt(a, b):
    return lax.dot_general(a, b, (((1,), (1,)), ((), ())), preferred_element_type=F32)


def _silu(x):
    hx = 0.5 * x
    return hx + hx * jnp.tanh(hx)


def _gelu_tanh(x):
    return 0.5 * x * (1.0 + jnp.tanh(0.7978845608028654 * (x + 0.044715 * (x * x * x))))


def _params(*sem):
    return pltpu.CompilerParams(dimension_semantics=sem, vmem_limit_bytes=VMEM_LIMIT)


def _const_spec(shape):
    nd = len(shape)
    return pl.BlockSpec(shape, lambda *_: (0,) * nd, pipeline_mode=pl.Buffered(1))


def _layer_spec(arr, l):
    nd = arr.ndim
    return pl.BlockSpec((None,) + arr.shape[1:], lambda *_: (l,) + (0,) * (nd - 1), pipeline_mode=pl.Buffered(1))


def _mod_spec(mods, l, which, row_map):
    d = mods.shape[-1]
    return pl.BlockSpec((None, None, None, 1, d), lambda i: (l, row_map(i), which, 0, 0))


def _ada_kernel(c_ref, w_ref, b_ref, o_ref):
    s = _silu(c_ref[...]).astype(BF)
    o_ref[...] = _dot(s, w_ref[...].astype(BF)) + b_ref[...]


def _ada_call(cvec, ada_w, ada_b):
    depth, d, d6 = ada_w.shape
    r = cvec.shape[0]
    tn = 1024
    return pl.pallas_call(
        _ada_kernel,
        grid=(depth, d6 // tn),
        in_specs=[
            pl.BlockSpec((r, d), lambda l, j: (0, 0)),
            pl.BlockSpec((None, d, tn), lambda l, j: (l, 0, j)),
            pl.BlockSpec((None, 1, tn), lambda l, j: (l, 0, j)),
        ],
        out_specs=pl.BlockSpec((None, r, tn), lambda l, j: (l, 0, j)),
        out_shape=jax.ShapeDtypeStruct((depth, r, d6), F32),
        compiler_params=_params("parallel", "parallel"),
        name="adaln",
    )(cvec, ada_w, ada_b.reshape(depth, 1, d6))


def _in_kernel(h_ref, sh_ref, sc_ref, g_ref, wall_ref, wqb_ref, wkvb_ref, gna_ref, gcq_ref, gckv_ref,
               gm_ref, cos_ref, sin_ref,
               q_ref, k_ref, v_ref, mq_ref, mk_ref, mv_ref, lx_ref, lg_ref):
    x = h_ref[...]
    y = x * lax.rsqrt(jnp.mean(x * x, axis=-1, keepdims=True) + EPS) * g_ref[...]
    y = y * (1.0 + sc_ref[...]) + sh_ref[...]
    u = _dot(y.astype(BF), wall_ref[...])
    tm = u.shape[0]
    lane = lax.broadcasted_iota(jnp.int32, (tm, LANE), 1)

    def seg_rsqrt(t, in_first, n_first, n_second):
        t2 = t * t
        s1 = jnp.sum(jnp.where(in_first, t2, 0.0), axis=-1, keepdims=True) * (1.0 / n_first)
        s2 = jnp.sum(jnp.where(in_first, 0.0, t2), axis=-1, keepdims=True) * (1.0 / n_second)
        return lax.rsqrt(jnp.where(in_first, s1, s2) + EPS)

    na_first = lane < NA_HEAD_DIM
    for p in range(2 * NA_WIDTH // LANE):
        lo = p * LANE
        t = u[:, C_Q + lo:C_Q + lo + LANE]
        tn = (t * seg_rsqrt(t, na_first, NA_HEAD_DIM, NA_HEAD_DIM) * gna_ref[:, lo:lo + LANE]).astype(BF)
        if lo < NA_WIDTH:
            q_ref[:, lo:lo + LANE] = tn
        else:
            k_ref[:, lo - NA_WIDTH:lo - NA_WIDTH + LANE] = tn
    v_ref[...] = u[:, C_V:C_V + NA_WIDTH].astype(BF)

    cos = cos_ref[...]
    sin = sin_ref[...]
    cq = u[:, C_CQ:C_CQ + MLA_Q_RANK]
    cqn = cq * lax.rsqrt(jnp.mean(cq * cq, axis=-1, keepdims=True) + EPS) * gcq_ref[...]
    qq = _dot(cqn.astype(BF), wqb_ref[...])
    a_q = gm_ref[0:1, :] * cos
    b_q = gm_ref[1:2, :] * sin
    nope = lane < MLA_NOPE
    for hd in range(MLA_HEADS):
        lo = hd * LANE
        qr = qq[:, lo:lo + LANE]
        qp = qq[:, MLA_QK_W + lo:MLA_QK_W + lo + LANE]
        mq_ref[:, lo:lo + LANE] = (seg_rsqrt(qr, nope, MLA_NOPE, MLA_ROPE) * (qr * a_q + qp * b_q)).astype(BF)

    ckv = u[:, C_CKV:C_CKV + MLA_KV_RANK]
    ckvn = ckv * lax.rsqrt(jnp.mean(ckv * ckv, axis=-1, keepdims=True) + EPS) * gckv_ref[...]
    kv = _dot(ckvn.astype(BF), wkvb_ref[...])
    krr = u[:, C_KR:C_KR + LANE]
    krp = u[:, C_KRP:C_KRP + LANE]
    msr = jnp.sum(krr * krr, axis=-1, keepdims=True) * (1.0 / MLA_ROPE)
    kr = lax.rsqrt(msr + EPS) * (krr * (gm_ref[3:4, :] * cos) + krp * (gm_ref[4:5, :] * sin))
    for hd in range(MLA_HEADS):
        lo = hd * LANE
        kn = kv[:, lo:lo + LANE]
        ms = jnp.sum(kn * kn, axis=-1, keepdims=True) * (1.0 / MLA_NOPE)
        mk_ref[:, lo:lo + LANE] = (kn * lax.rsqrt(ms + EPS) * gm_ref[2:3, :] + kr).astype(BF)
    mv_ref[...] = kv[:, MLA_QK_W:].astype(BF)

    lx_ref[...] = u[:, C_LX:C_LX + LRU_WIDTH]
    lg_ref[...] = u[:, C_LG:C_LG + LRU_WIDTH]


def _in_call(h, mods, row_map, l, wts, cos, sin, tm):
    n, d = h.shape
    tab_tiles = cos.shape[0] // tm
    row = lambda w: pl.BlockSpec((tm, w), lambda i: (i, 0))
    tab = pl.BlockSpec((tm, LANE), lambda i: (i % tab_tiles, 0))
    stacked = [wts["g_mix"], wts["w_all"], wts["w_qb"], wts["w_kvb"], wts["g_na"], wts["g_cq"], wts["g_ckv"],
               wts["g_mla"]]
    out_w =[(NA_WIDTH, BF), (NA_WIDTH, BF), (NA_WIDTH, BF), (MLA_QK_W, BF), (MLA_QK_W, BF),
             (MLA_HEADS * MLA_V, BF), (LRU_WIDTH, F32), (LRU_WIDTH, F32)]
    return pl.pallas_call(
        _in_kernel,
        grid=(n // tm,),
        in_specs=[row(d), _mod_spec(mods, l, 0, row_map), _mod_spec(mods, l, 1, row_map)]
                 + [_layer_spec(a, l) for a in stacked] + [tab, tab],
        out_specs=[row(w) for w, _ in out_w],
        out_shape=[jax.ShapeDtypeStruct((n, w), dt) for w, dt in out_w],
        compiler_params=_params("parallel"),
        name="in_proj",
    )(h, mods, mods, *stacked, cos, sin)


def _softmax_pv(s_a, s_b, v_a, v_b):
    m = jnp.maximum(jnp.max(s_a, axis=-1, keepdims=True), jnp.max(s_b, axis=-1, keepdims=True))
    p_a = jnp.exp2(s_a - m)
    p_b = jnp.exp2(s_b - m)
    l = jnp.sum(p_a, axis=-1, keepdims=True) + jnp.sum(p_b, axis=-1, keepdims=True)
    o = _dot(p_a.astype(BF), v_a) + _dot(p_b.astype(BF), v_b)
    return o / l


def _na_block_geometry(rows):
    nblk = rows // NA_QROWS
    half = NA_WIN_H // 2
    ks = np.clip(np.arange(nblk) * NA_QROWS - half, 0, rows - NA_KROWS)
    pats = []
    for j in range(nblk):
        r = j * NA_QROWS + np.arange(NA_QROWS)
        rs = np.clip(r - half, 0, rows - NA_WIN_H)
        assert ks[j] <= rs.min() and rs.max() + NA_WIN_H <= ks[j] + NA_KROWS
        pats.append(tuple(zip((r - ks[j]).tolist(), (rs - ks[j]).tolist())))
    cls = [0 if j == 0 else (2 if j == nblk - 1 else 1) for j in range(nblk)]
    by_cls = {}
    for j in range(nblk):
        assert by_cls.setdefault(cls[j], pats[j]) == pats[j]
    return [by_cls[c] for c in range(NA_CLASSES)]


def _na_kernel(q_ref, k_ref, v_ref, kc_ref, vc_ref, bias_ref, o_ref, *, rows):
    nblk = rows // NA_QROWS
    nq = NA_QROWS * GRID_W
    nk = NA_KROWS * GRID_W
    first_half = lax.broadcasted_iota(jnp.int32, (nq, LANE), 1) < NA_HEAD_DIM

    def block_geometry(sub):
        j = pl.program_id(1) * NA_BLOCKS_PER_STEP + sub
        ks = jnp.clip(j * NA_QROWS - NA_WIN_H // 2, 0, rows - NA_KROWS)
        cls = jnp.where(j == 0, 0, jnp.where(j == nblk - 1, 2, 1))
        return pl.multiple_of(ks * GRID_W, GRID_W), cls

    geom = [block_geometry(sub) for sub in range(NA_BLOCKS_PER_STEP)]

    def scores(sub, hd):
        start, cls = geom[sub]
        lo = (hd // 2) * LANE
        q2 = q_ref[sub * nq:(sub + 1) * nq, lo:lo + LANE]
        qm = jnp.where(first_half if hd % 2 == 0 else ~first_half, q2, jnp.zeros_like(q2))
        s_w = _dot_nt(qm, k_ref[pl.ds(start, nk), lo:lo + LANE]) + bias_ref[cls, hd]
        return s_w, _dot_nt(qm, kc_ref[:, lo:lo + LANE])

    work = [(sub, hd) for sub in range(NA_BLOCKS_PER_STEP) for hd in range(NA_HEADS)]
    nxt = scores(*work[0])
    outs = []
    for i, (sub, hd) in enumerate(work):
        s_w, s_c = nxt
        if i + 1 < len(work):
            nxt = scores(*work[i + 1])
        start, _ = geom[sub]
        lo = (hd // 2) * LANE
        outs.append(_softmax_pv(s_w, s_c, v_ref[pl.ds(start, nk), lo:lo + LANE], vc_ref[:, lo:lo + LANE]))
        if hd % 2 == 1:
            o_ref[sub * nq:(sub + 1) * nq, lo:lo + LANE] = jnp.where(first_half, outs[0], outs[1]).astype(BF)
            outs = []


def _na_call(q, k, v, kc, vc, bias, l, batch):
    n, w = q.shape
    s = n // batch
    c = kc.shape[0] // batch
    rows = s // GRID_W
    q3, k3, v3 = (a.reshape(batch, s, w) for a in (q, k, v))
    kc3, vc3 = (a.reshape(batch, c, w) for a in (kc, vc))
    nq = NA_BLOCKS_PER_STEP * NA_QROWS * GRID_W
    out = pl.pallas_call(
        functools.partial(_na_kernel, rows=rows),
        grid=(batch, rows // (NA_BLOCKS_PER_STEP * NA_QROWS)),
        in_specs=[
            pl.BlockSpec((None, nq, w), lambda b, j: (b, j, 0)),
            pl.BlockSpec((None, s, w), lambda b, j: (b, 0, 0)),
            pl.BlockSpec((None, s, w), lambda b, j: (b, 0, 0)),
            pl.BlockSpec((None, c, w), lambda b, j: (b, 0, 0)),
            pl.BlockSpec((None, c, w), lambda b, j: (b, 0, 0)),
            _layer_spec(bias, l),
        ],
        out_specs=pl.BlockSpec((None, nq, w), lambda b, j: (b, j, 0)),
        out_shape=jax.ShapeDtypeStruct((batch, s, w), BF),
        compiler_params=_params("parallel", "arbitrary"),
        name="nbr_attn",
    )(q3, k3, v3, kc3, vc3, bias)
    return out.reshape(n, w)


def _na_bias_tables(rpb, rows):
    depth = rpb.shape[0]
    qc = np.arange(GRID_W)
    kcol = np.arange(GRID_W)
    cs = np.clip(qc - NA_WIN_W // 2, 0, GRID_W - NA_WIN_W)
    ok = (kcol[None, :] >= cs[:, None]) & (kcol[None, :] < cs[:, None] + NA_WIN_W)
    col_rel = np.clip(kcol[None, :] - qc[:, None], 1 - NA_WIN_W, NA_WIN_W - 1) + NA_WIN_W - 1
    sel = (col_rel[None] == np.arange(2 * NA_WIN_W - 1)[:, None, None]).astype(np.float32)
    t = jnp.einsum("lhrd,dqk->lhrqk", rpb.astype(F32), jnp.asarray(sel), precision=lax.Precision.HIGHEST)
    t = jnp.where(ok, t * LOG2E, NEG_BIG)
    neg = jnp.full((depth, NA_HEADS, GRID_W, GRID_W), NEG_BIG, F32)
    classes = []
    for pat in _na_block_geometry(rows):
        qrows = []
        for qo, wo in pat:
            blocks = [t[:, :, kr - qo + NA_WIN_H - 1] if wo <= kr < wo + NA_WIN_H else neg for kr in range(NA_KROWS)]
            qrows.append(jnp.concatenate(blocks, axis=-1))
        classes.append(jnp.concatenate(qrows, axis=-2))
    return jnp.stack(classes, axis=1)


MLA_TILES_PER_STEP = 2


def _mla_kernel(q_ref, k_ref, v_ref, kc_ref, vc_ref, o_ref, *, tq):
    first_half = lax.broadcasted_iota(jnp.int32, (tq, LANE), 1) < MLA_V

    def scores(sub, hd):
        hl = hd * LANE
        qh = q_ref[sub * tq:(sub + 1) * tq, hl:hl + LANE]
        return _dot_nt(qh, k_ref[:, hl:hl + LANE]), _dot_nt(qh, kc_ref[:, hl:hl + LANE])

    work = [(sub, hd) for sub in range(q_ref.shape[0] // tq) for hd in range(MLA_HEADS)]
    nxt = scores(*work[0])
    outs = []
    for i, (sub, hd) in enumerate(work):
        s_l, s_c = nxt
        if i + 1 < len(work):
            nxt = scores(*work[i + 1])
        lo = (hd // 2) * LANE
        outs.append(_softmax_pv(s_l, s_c, v_ref[:, lo:lo + LANE], vc_ref[:, lo:lo + LANE]))
        if hd % 2 == 1:
            o_ref[sub * tq:(sub + 1) * tq, lo:lo + LANE] = jnp.where(first_half, outs[0], outs[1]).astype(BF)
            outs = []


def _mla_call(q, k, v, kc, vc, batch, tq):
    n, wq = q.shape
    wv = v.shape[1]
    s = n // batch
    c = kc.shape[0] // batch
    q3, k3 = q.reshape(batch, s, wq), k.reshape(batch, s, wq)
    v3 = v.reshape(batch, s, wv)
    kc3, vc3 = kc.reshape(batch, c, wq), vc.reshape(batch, c, wv)
    step_q = MLA_TILES_PER_STEP * tq
    out = pl.pallas_call(
        functools.partial(_mla_kernel, tq=tq),
        grid=(batch, s // step_q),
        in_specs=[
            pl.BlockSpec((None, step_q, wq), lambda b, i: (b, i, 0)),
            pl.BlockSpec((None, s, wq), lambda b, i: (b, 0, 0)),
            pl.BlockSpec((None, s, wv), lambda b, i: (b, 0, 0)),
            pl.BlockSpec((None, c, wq), lambda b, i: (b, 0, 0)),
            pl.BlockSpec((None, c, wv), lambda b, i: (b, 0, 0)),
        ],
        out_specs=pl.BlockSpec((None, step_q, wv), lambda b, i: (b, i, 0)),
        out_shape=jax.ShapeDtypeStruct((batch, s, wv), BF),
        compiler_params=_params("parallel", "arbitrary"),
        name="latent_attn",
    )(q3, k3, v3, kc3, vc3)
    return out.reshape(n, wv)


def _softmax_pv1(s, v):
    m = jnp.max(s, axis=-1, keepdims=True)
    p = jnp.exp2(s - m)
    l = jnp.sum(p, axis=-1, keepdims=True)
    return _dot(p.astype(BF), v) / l


def _ctx_attn_kernel(q_ref, k_ref, v_ref, mq_ref, mk_ref, mv_ref, ona_ref, omla_ref):
    c = q_ref.shape[0]
    first_half = lax.broadcasted_iota(jnp.int32, (c, LANE), 1) < NA_HEAD_DIM
    for p in range(NA_WIDTH // LANE):
        lo = p * LANE
        q2 = q_ref[:, lo:lo + LANE]
        k2 = k_ref[:, lo:lo + LANE]
        v2 = v_ref[:, lo:lo + LANE]
        outs = []
        for hh in range(2):
            qm = jnp.where(first_half if hh == 0 else ~first_half, q2, jnp.zeros_like(q2))
            outs.append(_softmax_pv1(_dot_nt(qm, k2), v2))
        ona_ref[:, lo:lo + LANE] = jnp.where(first_half, outs[0], outs[1]).astype(BF)
    for p in range(MLA_HEADS // 2):
        lo = p * LANE
        v2 = mv_ref[:, lo:lo + LANE]
        outs = []
        for hh in range(2):
            hl = (2 * p + hh) * LANE
            outs.append(_softmax_pv1(_dot_nt(mq_ref[:, hl:hl + LANE], mk_ref[:, hl:hl + LANE]), v2))
        omla_ref[:, lo:lo + LANE] = jnp.where(first_half, outs[0], outs[1]).astype(BF)


def _ctx_attn_call(q, k, v, mq, mk, mv, batch):
    nc = q.shape[0]
    c = nc // batch
    ins = [q, k, v, mq, mk, mv]
    ins3 = [a.reshape(batch, c, a.shape[1]) for a in ins]
    spec = lambda w: pl.BlockSpec((None, c, w), lambda b: (b, 0, 0))
    ona, omla = pl.pallas_call(
        _ctx_attn_kernel,
        grid=(batch,),
        in_specs=[spec(a.shape[1]) for a in ins],
        out_specs=[spec(NA_WIDTH), spec(MLA_HEADS * MLA_V)],
        out_shape=[jax.ShapeDtypeStruct((batch, c, NA_WIDTH), BF),
                   jax.ShapeDtypeStruct((batch, c, MLA_HEADS * MLA_V), BF)],
        compiler_params=_params("parallel"),
        name="ctx_attn",
    )(*ins3)
    return ona.reshape(nc, NA_WIDTH), omla.reshape(nc, MLA_HEADS * MLA_V)


LRU_PAD = SUBLANE
LRU_CHUNK = 512


def _lru_coeffs(x_ref, n, xpad_ref, a_ref, u_ref, cw_ref, cb_ref, wg_ref, bg_ref, sp2):
    w = LRU_WIDTH
    zeros = jnp.zeros((LRU_PAD, w), F32)
    xpad_ref[0:LRU_PAD, :] = zeros
    xpad_ref[LRU_PAD:LRU_PAD + n, :] = x_ref[...]
    xpad_ref[LRU_PAD + n:2 * LRU_PAD + n, :] = zeros
    left = (LRU_CONV_W - 1) // 2
    for c0 in range(0, n, LRU_CHUNK):
        cn = min(LRU_CHUNK, n - c0)
        xc = jnp.zeros((cn, w), F32) + cb_ref[...]
        for j in range(LRU_CONV_W):
            off = LRU_PAD + c0 + j - left
            xc = xc + cw_ref[j:j + 1, :] * xpad_ref[off:off + cn, :]
        t = jnp.tanh(_dot(xc.astype(BF), wg_ref[...]) + bg_ref[...])
        xh = 0.5 * xc
        for dr in range(2):
            a = jnp.exp2(sp2[:, dr * w:(dr + 1) * w] * (1.0 + t[:, dr * w:(dr + 1) * w]))
            a_ref[dr, c0:c0 + cn, :] = a
            u_ref[dr, c0:c0 + cn, :] = jnp.sqrt(1.0 - a * a) * ((1.0 + t[:, (2 + dr) * w:(3 + dr) * w]) * xh)


def _lru_scan(n, a_ref, u_ref, hf0, hb0):
    w = LRU_WIDTH
    nchunk = n // SUBLANE
    rowi = lax.broadcasted_iota(jnp.int32, (SUBLANE, w), 0)

    def body(c, carry):
        hf, hb = carry
        rf = pl.multiple_of(c * SUBLANE, SUBLANE)
        rb = pl.multiple_of((nchunk - 1 - c) * SUBLANE, SUBLANE)
        af = a_ref[0, pl.ds(rf, SUBLANE), :]
        uf = u_ref[0, pl.ds(rf, SUBLANE), :]
        ab = a_ref[1, pl.ds(rb, SUBLANE), :]
        ub = u_ref[1, pl.ds(rb, SUBLANE), :]
        for s in (1, 2, 4):
            mf = rowi >= s
            uf = jnp.where(mf, af * pltpu.roll(uf, s, 0) + uf, uf)
            af = jnp.where(mf, af * pltpu.roll(af, s, 0), af)
            mb = rowi < SUBLANE - s
            ub = jnp.where(mb, ab * pltpu.roll(ub, SUBLANE - s, 0) + ub, ub)
            ab = jnp.where(mb, ab * pltpu.roll(ab, SUBLANE - s, 0), ab)
        hfc = af * hf + uf
        hbc = ab * hb + ub
        u_ref[0, pl.ds(rf, SUBLANE), :] = hfc
        u_ref[1, pl.ds(rb, SUBLANE), :] = hbc
        return hfc[SUBLANE - 1:SUBLANE, :], hbc[0:1, :]

    return lax.fori_loop(0, nchunk, body, (hf0, hb0))


def _lru_kernel(lx_ref, lg_ref, clx_ref, clg_ref, cw_ref, cb_ref, wg_ref, bg_ref, lam_ref,
                o_ref, oc_ref, xpad_ref, a_ref, u_ref, ac_ref, uc_ref):
    n = lx_ref.shape[0]
    nc = clx_ref.shape[0]
    nlam = -lam_ref[...]
    sp = jnp.maximum(nlam, 0.0) + jnp.log(1.0 + jnp.exp(-jnp.abs(nlam)))
    sp2 = (-0.5 * LRU_C * LOG2E) * sp
    zero = jnp.zeros((1, LRU_WIDTH), F32)
    _lru_coeffs(clx_ref, nc, xpad_ref, ac_ref, uc_ref, cw_ref, cb_ref, wg_ref, bg_ref, sp2)
    hf0, hb0 = _lru_scan(nc, ac_ref, uc_ref, zero, zero)
    oc_ref[...] = (_gelu_tanh(clg_ref[...]) * (uc_ref[0] + uc_ref[1])).astype(BF)
    _lru_coeffs(lx_ref, n, xpad_ref, a_ref, u_ref, cw_ref, cb_ref, wg_ref, bg_ref, sp2)
    _lru_scan(n, a_ref, u_ref, hf0, hb0)
    for c0 in range(0, n, LRU_CHUNK):
        cn = min(LRU_CHUNK, n - c0)
        hsum = u_ref[0, c0:c0 + cn, :] + u_ref[1, c0:c0 + cn, :]
        o_ref[c0:c0 + cn, :] = (_gelu_tanh(lg_ref[c0:c0 + cn, :]) * hsum).astype(BF)


def _lru_call(lx, lg, clx, clg, l, wts, batch):
    n, w = lx.shape
    s = n // batch
    c = clx.shape[0] // batch
    seq = lambda a, ln: a.reshape(batch, ln, w)
    lat = pl.BlockSpec((None, s, w), lambda b: (b, 0, 0))
    ctx = pl.BlockSpec((None, c, w), lambda b: (b, 0, 0))
    stacked = [wts["conv_w"], wts["conv_b"], wts["w_gate"], wts["b_gate"], wts["lam"]]
    o, oc = pl.pallas_call(
        _lru_kernel,
        grid=(batch,),
        in_specs=[lat, lat, ctx, ctx] + [_layer_spec(a, l) for a in stacked],
        out_specs=[lat, ctx],
        out_shape=[jax.ShapeDtypeStruct((batch, s, w), BF), jax.ShapeDtypeStruct((batch, c, w), BF)],
        scratch_shapes=[
            pltpu.VMEM((s + 2 * LRU_PAD, w), F32),
            pltpu.VMEM((2, s, w), F32),
            pltpu.VMEM((2, s, w), F32),
            pltpu.VMEM((2, c, w), F32),
            pltpu.VMEM((2, c, w), F32),
        ],
        compiler_params=_params("parallel"),
        name="rglru",
    )(seq(lx, s), seq(lg, s), seq(clx, c), seq(clg, c), *stacked)
    return o.reshape(n, w), oc.reshape(batch * c, w)


def _out_kernel(h_ref, na_ref, mla_ref, lru_ref, gm_ref, shf_ref, scf_ref, gf_ref, g_ref,
                wo_ref, wg_ref, wu_ref, wd_ref, o_ref, *, hid_chunk, sub_rows):
    hidden = wg_ref.shape[1]

    def residual_and_norm(r0):
        rs = slice(r0, r0 + sub_rows)
        o_cat = jnp.concatenate([na_ref[rs, :], mla_ref[rs, :], lru_ref[rs, :]], axis=-1)
        h1 = h_ref[rs, :] + gm_ref[...] * _dot(o_cat, wo_ref[...])
        y = h1 * lax.rsqrt(jnp.mean(h1 * h1, axis=-1, keepdims=True) + EPS) * g_ref[...]
        return h1, (y * (1.0 + scf_ref[...]) + shf_ref[...]).astype(BF)

    def gate_up(y, c0):
        return _dot(y, wg_ref[:, c0:c0 + hid_chunk]), _dot(y, wu_ref[:, c0:c0 + hid_chunk])

    starts = list(range(0, h_ref.shape[0], sub_rows))
    h1, y = residual_and_norm(starts[0])
    nxt = gate_up(y, 0)
    for si, r0 in enumerate(starts):
        acc = None
        h1_next = y_next = None
        for c0 in range(0, hidden, hid_chunk):
            gate, up = nxt
            if c0 + hid_chunk < hidden:
                nxt = gate_up(y, c0 + hid_chunk)
            elif si + 1 < len(starts):
                h1_next, y_next = residual_and_norm(starts[si + 1])
                nxt = gate_up(y_next, 0)
            d = _dot((_silu(gate) * up).astype(BF), wd_ref[c0:c0 + hid_chunk, :])
            acc = d if acc is None else acc + d
        o_ref[r0:r0 + sub_rows, :] = h1 + gf_ref[...] * acc
        h1, y = h1_next, y_next


def _out_call(h, na_o, mla_o, lru_o, mods, row_map, l, wts, tm, sub_rows):
    n, d = h.shape
    row = lambda w: pl.BlockSpec((tm, w), lambda i: (i, 0))
    stacked = [wts["g_ffn"], wts["w_out"], wts["ffn_gate"], wts["ffn_up"], wts["ffn_down"]]
    hidden = wts["ffn_gate"].shape[-1]
    hid_chunk = 2 * LANE if hidden % (2 * LANE) == 0 else hidden
    return pl.pallas_call(
        functools.partial(_out_kernel, hid_chunk=hid_chunk, sub_rows=sub_rows),
        grid=(n // tm,),
        in_specs=[row(d), row(na_o.shape[1]), row(mla_o.shape[1]), row(lru_o.shape[1])]
                 + [_mod_spec(mods, l, which, row_map) for which in (2, 3, 4, 5)]
                 + [_layer_spec(a, l) for a in stacked],
        out_specs=row(d),
        out_shape=jax.ShapeDtypeStruct((n, d), F32),
        compiler_params=_params("parallel"),
        name="out_ffn",
    )(h, na_o, mla_o, lru_o, mods, mods, mods, mods, *stacked)


def _rope_perm():
    j = np.arange(MLA_ROPE)
    first = (j % (MLA_ROPE // 2)) < MLA_ROPE // 4
    partner = np.where(first, j + MLA_ROPE // 4, j - MLA_ROPE // 4)
    sign = np.where(first, -1.0, 1.0).astype(np.float32)
    return partner, sign


def _rope_tables(s):
    t = np.arange(s)
    row = (t // GRID_W).astype(np.float32)
    col = (t % GRID_W).astype(np.float32)
    n_freq = MLA_ROPE // 4
    inv = jnp.asarray(ROPE_BASE, F32) ** (-jnp.arange(n_freq, dtype=F32) / n_freq)
    ar = jnp.asarray(row)[:, None] * inv
    ac = jnp.asarray(col)[:, None] * inv
    ang = jnp.concatenate([ar, ar, ac, ac], axis=-1)
    pad = LANE - MLA_NOPE - MLA_ROPE
    cos = jnp.concatenate([jnp.ones((s, MLA_NOPE), F32), jnp.cos(ang), jnp.zeros((s, pad), F32)], axis=-1)
    sin = jnp.concatenate([jnp.zeros((s, MLA_NOPE), F32), jnp.sin(ang), jnp.zeros((s, pad), F32)], axis=-1)
    return cos, sin


def _lane_tile(nope, rope):
    lead = nope.shape[:-1]
    pad = jnp.zeros(lead + (LANE - MLA_NOPE - MLA_ROPE,), nope.dtype)
    return jnp.concatenate([nope, rope, pad], axis=-1)


def _prep_weights(p):
    partner, sign = _rope_perm()
    w_in = p["w_in"]
    depth, d, _ = w_in.shape
    cols = np.cumsum((NA_WIDTH, NA_WIDTH, NA_WIDTH, MLA_Q_RANK, MLA_KV_RANK, MLA_ROPE, LRU_WIDTH, LRU_WIDTH))
    w_kr = w_in[..., cols[4]:cols[5]]
    zn = jnp.zeros((depth, d, MLA_NOPE), F32)
    w_all = jnp.concatenate([
        w_in[..., :cols[4]], _lane_tile(zn, w_kr), _lane_tile(zn, w_kr[..., partner]), w_in[..., cols[5]:]], axis=-1)
    assert w_all.shape[-1] == W_ALL

    wq = p["mla_w_qb"].reshape(depth, MLA_Q_RANK, MLA_HEADS, MLA_NOPE + MLA_ROPE)
    wq_n, wq_r = wq[..., :MLA_NOPE], wq[..., MLA_NOPE:]
    w_qb = jnp.concatenate([
        _lane_tile(wq_n, wq_r).reshape(depth, MLA_Q_RANK, MLA_QK_W),
        _lane_tile(jnp.zeros_like(wq_n), wq_r[..., partner]).reshape(depth, MLA_Q_RANK, MLA_QK_W)], axis=-1)

    wkv = p["mla_w_kvb"].reshape(depth, MLA_KV_RANK, MLA_HEADS, MLA_NOPE + MLA_V)
    wk_n, wv = wkv[..., :MLA_NOPE], wkv[..., MLA_NOPE:]
    w_kvb = jnp.concatenate([
        _lane_tile(wk_n, jnp.zeros(wk_n.shape[:-1] + (MLA_ROPE,), F32)).reshape(depth, MLA_KV_RANK, MLA_QK_W),
        wv.reshape(depth, MLA_KV_RANK, MLA_HEADS * MLA_V)], axis=-1)

    qg, kg = p["mla_q_g"], p["mla_k_g"]
    z_n, z_r = jnp.zeros((depth, MLA_NOPE), F32), jnp.zeros((depth, MLA_ROPE), F32)
    sgn = jnp.asarray(sign)
    zrow = jnp.zeros((depth, LANE), F32)
    g_mla = jnp.stack([
        _lane_tile(qg[:, :MLA_NOPE], qg[:, MLA_NOPE:]) * MLA_SCALE,
        _lane_tile(z_n, sgn * qg[:, MLA_NOPE:][:, partner]) * MLA_SCALE,
        _lane_tile(kg[:, :MLA_NOPE], z_r),
        _lane_tile(z_n, kg[:, MLA_NOPE:]),
        _lane_tile(z_n, sgn * kg[:, MLA_NOPE:][:, partner]),
        zrow, zrow, zrow], axis=1)

    g_na = jnp.concatenate([jnp.tile(p["na_q_g"], (1, NA_HEADS)) * NA_SCALE, jnp.tile(p["na_k_g"], (1, NA_HEADS))],
                           axis=-1)[:, None]

    wa, wx = p["lru_w_a"], p["lru_w_x"]
    wg4 = jnp.stack([wa[:, 0], wa[:, 1], wx[:, 0], wx[:, 1]], axis=1)
    eye = jnp.eye(LRU_HEADS, dtype=F32)
    w_gate = (wg4[:, :, :, :, None, :] * eye[None, None, :, None, :, None])
    w_gate = w_gate.transpose(0, 2, 3, 1, 4, 5).reshape(depth, LRU_WIDTH, 4 * LRU_WIDTH)
    ba, bx = p["lru_b_a"], p["lru_b_x"]
    b_gate = jnp.concatenate([ba[:, 0], ba[:, 1], bx[:, 0], bx[:, 1]], axis=-1)[:, None]

    return {
        "g_mix": p["norm_mix_g"][:, None], "g_ffn": p["norm_ffn_g"][:, None],
        "w_all": w_all.astype(BF), "w_qb": w_qb.astype(BF), "w_kvb": w_kvb.astype(BF),
        "g_na": g_na, "g_cq": p["mla_cq_g"][:, None], "g_ckv": p["mla_ckv_g"][:, None], "g_mla": g_mla,
        "conv_w": p["lru_conv_w"], "conv_b": p["lru_conv_b"][:, None],
        "w_gate": (0.5 * w_gate).astype(BF), "b_gate": 0.5 * b_gate, "lam": p["lru_lambda"].reshape(depth, 1, 2 * LRU_WIDTH),
        "w_out": p["w_out"].astype(BF), "ffn_gate": p["ffn_w_gate"].astype(BF),
        "ffn_up": p["ffn_w_up"].astype(BF), "ffn_down": p["ffn_w_down"].astype(BF),
    }


def kernel(x, c, ctx, c_ctx, ada_w, ada_b, norm_mix_g, norm_ffn_g, w_in, na_q_g, na_k_g, na_rpb, mla_cq_g, mla_w_qb, mla_ckv_g, mla_w_kvb, mla_q_g, mla_k_g, lru_conv_w, lru_conv_b, lru_w_a, lru_b_a, lru_w_x, lru_b_x, lru_lambda, w_out, ffn_w_gate, ffn_w_up, ffn_w_down):
    batch, s, d = x.shape
    cl = ctx.shape[1]
    depth = ada_w.shape[0]
    rows = s // GRID_W
    assert s % GRID_W == 0 and rows % (NA_BLOCKS_PER_STEP * NA_QROWS) == 0 and rows >= 2 * NA_KROWS
    p = dict(norm_mix_g=norm_mix_g, norm_ffn_g=norm_ffn_g, w_in=w_in, na_q_g=na_q_g, na_k_g=na_k_g,
             mla_cq_g=mla_cq_g, mla_w_qb=mla_w_qb, mla_ckv_g=mla_ckv_g, mla_w_kvb=mla_w_kvb, mla_q_g=mla_q_g,
             mla_k_g=mla_k_g, lru_conv_w=lru_conv_w, lru_conv_b=lru_conv_b, lru_w_a=lru_w_a, lru_b_a=lru_b_a,
             lru_w_x=lru_w_x, lru_b_x=lru_b_x, lru_lambda=lru_lambda, w_out=w_out, ffn_w_gate=ffn_w_gate,
             ffn_w_up=ffn_w_up, ffn_w_down=ffn_w_down)

    tm = 512 if s % 512 == 0 else GRID_W * NA_QROWS
    tmc = tm if (batch * cl) % tm == 0 else cl
    tq = 512

    nmod = -(-(batch + 1) // SUBLANE) * SUBLANE
    cvec = jnp.concatenate([c, c_ctx[None], jnp.zeros((nmod - batch - 1, d), F32)], axis=0)
    mods = _ada_call(cvec, ada_w, ada_b).reshape(depth, nmod, 6, 1, d)
    tiles_per_batch = s // tm
    lat_row = lambda i: i // tiles_per_batch
    ctx_row = lambda i: batch
    tm_out = 2 * tm if s % (2 * tm) == 0 else tm
    tmc_out = 2 * tmc if (batch * cl) % (2 * tmc) == 0 else tmc
    out_tiles_per_batch = s // tm_out
    lat_row_out = lambda i: i // out_tiles_per_batch

    wts = _prep_weights(p)
    na_bias = _na_bias_tables(na_rpb, rows)
    cos, sin = _rope_tables(s)
    cos_c = jnp.concatenate([jnp.ones((tmc, MLA_NOPE + MLA_ROPE), F32),
                             jnp.zeros((tmc, LANE - MLA_NOPE - MLA_ROPE), F32)], axis=-1)
    sin_c = jnp.zeros((tmc, LANE), F32)

    h = x.reshape(batch * s, d)
    hc = ctx.reshape(batch * cl, d)
    for l in range(depth):
        last = l == depth - 1
        q, k, v, mq, mk, mv, lx, lg = _in_call(h, mods, lat_row, l, wts, cos, sin, tm)
        cq_, ck, cv, cmq, cmk, cmv, clx, clg = _in_call(hc, mods, ctx_row, l, wts, cos_c, sin_c, tmc)

        na_o = _na_call(q, k, v, ck, cv, na_bias, l, batch)
        mla_o = _mla_call(mq, mk, mv, cmk, cmv, batch, tq)
        lru_o, lru_c = _lru_call(lx, lg, clx, clg, l, wts, batch)
        h = _out_call(h, na_o, mla_o, lru_o, mods, lat_row_out, l, wts, tm_out, tm)

        if not last:
            na_c, mla_c = _ctx_attn_call(cq_, ck, cv, cmq, cmk, cmv, batch)
            hc = _out_call(hc, na_c, mla_c, lru_c, mods, ctx_row, l, wts, tmc_out, tmc)
    return h.reshape(batch, s, d)
```

```python
import functools
import math

import numpy as np
import jax
import jax.numpy as jnp
from jax import lax
from jax.experimental import pallas as pl
from jax.experimental.pallas import tpu as pltpu

GRID_W = 64
NA_HEADS = 6
NA_HEAD_DIM = 64
NA_WIDTH = NA_HEADS * NA_HEAD_DIM
NA_WIN_H = 8
NA_WIN_W = 16
MLA_HEADS = 6
MLA_Q_RANK = 256
MLA_KV_RANK = 128
MLA_NOPE = 64
MLA_ROPE = 32
MLA_V = 64
LRU_WIDTH = 256
LRU_HEADS = 4
LRU_BLOCK = LRU_WIDTH // LRU_HEADS
LRU_CONV_W = 4
LRU_C = 8.0
ROPE_BASE = 10000.0
EPS = 1e-6
LOG2E = math.log2(math.e)
NA_SCALE = NA_HEAD_DIM ** -0.5 * LOG2E
MLA_SCALE = (MLA_NOPE + MLA_ROPE) ** -0.5 * LOG2E

LANE = 128
SUBLANE = 8
MLA_QK_W = MLA_HEADS * LANE
NEG_BIG = -1e30
VMEM_LIMIT = 56 * 1024 * 1024

NA_QROWS = 4
NA_KROWS = NA_QROWS + NA_WIN_H
NA_CLASSES = 3
NA_BLOCKS_PER_STEP = 4

C_Q, C_K, C_V = 0, NA_WIDTH, 2 * NA_WIDTH
C_CQ = 3 * NA_WIDTH
C_CKV = C_CQ + MLA_Q_RANK
C_KR = C_CKV + MLA_KV_RANK
C_KRP = C_KR + LANE
C_LX = C_KRP + LANE
C_LG = C_LX + LRU_WIDTH
W_ALL = C_LG + LRU_WIDTH

BF = jnp.bfloat16
F32 = jnp.float32


def _dot(a, b):
    return jnp.dot(a, b, preferred_element_type=F32)


def _dot_nt(a, b):
    return lax.dot_general(a, b, (((1,), (1,)), ((), ())), preferred_element_type=F32)


def _silu(x):
    hx = 0.5 * x
    return hx + hx * jnp.tanh(hx)


def _gelu_tanh(x):
    return 0.5 * x * (1.0 + jnp.tanh(0.7978845608028654 * (x + 0.044715 * (x * x * x))))


def _params(*sem):
    return pltpu.CompilerParams(dimension_semantics=sem, vmem_limit_bytes=VMEM_LIMIT)


def _const_spec(shape):
    nd = len(shape)
    return pl.BlockSpec(shape, lambda *_: (0,) * nd, pipeline_mode=pl.Buffered(1))


def _layer_spec(arr, l):
    nd = arr.ndim
    return pl.BlockSpec((None,) + arr.shape[1:], lambda *_: (l,) + (0,) * (nd - 1), pipeline_mode=pl.Buffered(1))


def _mod_spec(mods, l, which, row_map):
    d = mods.shape[-1]
    return pl.BlockSpec((None, None, None, 1, d), lambda i: (l, row_map(i), which, 0, 0))


def _ada_kernel(c_ref, w_ref, b_ref, o_ref):
    s = _silu(c_ref[...]).astype(BF)
    o_ref[...] = _dot(s, w_ref[...].astype(BF)) + b_ref[...]


def _ada_call(cvec, ada_w, ada_b):
    depth, d, d6 = ada_w.shape
    r = cvec.shape[0]
    tn = 1024
    return pl.pallas_call(
        _ada_kernel,
        grid=(depth, d6 // tn),
        in_specs=[
            pl.BlockSpec((r, d), lambda l, j: (0, 0)),
            pl.BlockSpec((None, d, tn), lambda l, j: (l, 0, j)),
            pl.BlockSpec((None, 1, tn), lambda l, j: (l, 0, j)),
        ],
        out_specs=pl.BlockSpec((None, r, tn), lambda l, j: (l, 0, j)),
        out_shape=jax.ShapeDtypeStruct((depth, r, d6), F32),
        compiler_params=_params("parallel", "parallel"),
        name="adaln",
    )(cvec, ada_w, ada_b.reshape(depth, 1, d6))


def _in_kernel(h_ref, sh_ref, sc_ref, g_ref, wall_ref, wqb_ref, wkvb_ref, gna_ref, gcq_ref, gckv_ref,
               gm_ref, cos_ref, sin_ref,
               q_ref, k_ref, v_ref, mq_ref, mk_ref, mv_ref, lx_ref, lg_ref, *, sub_rows):
    lane = lax.broadcasted_iota(jnp.int32, (sub_rows, LANE), 1)
    na_first = lane < NA_HEAD_DIM
    nope = lane < MLA_NOPE

    def seg_rsqrt(t, in_first, n_first, n_second):
        t2 = t * t
        s1 = jnp.sum(jnp.where(in_first, t2, 0.0), axis=-1, keepdims=True) * (1.0 / n_first)
        s2 = jnp.sum(jnp.where(in_first, 0.0, t2), axis=-1, keepdims=True) * (1.0 / n_second)
        return lax.rsqrt(jnp.where(in_first, s1, s2) + EPS)

    def project(r0):
        x = h_ref[r0:r0 + sub_rows, :]
        y = x * lax.rsqrt(jnp.mean(x * x, axis=-1, keepdims=True) + EPS) * g_ref[...]
        y = y * (1.0 + sc_ref[...]) + sh_ref[...]
        return _dot(y.astype(BF), wall_ref[...])

    def head_groups(r0, u):
        rs = slice(r0, r0 + sub_rows)
        for p in range(2 * NA_WIDTH // LANE):
            lo = p * LANE
            t = u[:, C_Q + lo:C_Q + lo + LANE]
            tn = (t * seg_rsqrt(t, na_first, NA_HEAD_DIM, NA_HEAD_DIM) * gna_ref[:, lo:lo + LANE]).astype(BF)
            if lo < NA_WIDTH:
                q_ref[rs, lo:lo + LANE] = tn
            else:
                k_ref[rs, lo - NA_WIDTH:lo - NA_WIDTH + LANE] = tn
        v_ref[rs, :] = u[:, C_V:C_V + NA_WIDTH].astype(BF)

        cos = cos_ref[rs, :]
        sin = sin_ref[rs, :]
        cq = u[:, C_CQ:C_CQ + MLA_Q_RANK]
        cqn = cq * lax.rsqrt(jnp.mean(cq * cq, axis=-1, keepdims=True) + EPS) * gcq_ref[...]
        qq = _dot(cqn.astype(BF), wqb_ref[...])
        a_q = gm_ref[0:1, :] * cos
        b_q = gm_ref[1:2, :] * sin
        for hd in range(MLA_HEADS):
            lo = hd * LANE
            qr = qq[:, lo:lo + LANE]
            qp = qq[:, MLA_QK_W + lo:MLA_QK_W + lo + LANE]
            mq_ref[rs, lo:lo + LANE] = (seg_rsqrt(qr, nope, MLA_NOPE, MLA_ROPE) * (qr * a_q + qp * b_q)).astype(BF)

        ckv = u[:, C_CKV:C_CKV + MLA_KV_RANK]
        ckvn = ckv * lax.rsqrt(jnp.mean(ckv * ckv, axis=-1, keepdims=True) + EPS) * gckv_ref[...]
        kv = _dot(ckvn.astype(BF), wkvb_ref[...])
        krr = u[:, C_KR:C_KR + LANE]
        krp = u[:, C_KRP:C_KRP + LANE]
        msr = jnp.sum(krr * krr, axis=-1, keepdims=True) * (1.0 / MLA_ROPE)
        kr = lax.rsqrt(msr + EPS) * (krr * (gm_ref[3:4, :] * cos) + krp * (gm_ref[4:5, :] * sin))
        for hd in range(MLA_HEADS):
            lo = hd * LANE
            kn = kv[:, lo:lo + LANE]
            ms = jnp.sum(kn * kn, axis=-1, keepdims=True) * (1.0 / MLA_NOPE)
            mk_ref[rs, lo:lo + LANE] = (kn * lax.rsqrt(ms + EPS) * gm_ref[2:3, :] + kr).astype(BF)
        mv_ref[rs, :] = kv[:, MLA_QK_W:].astype(BF)

        lx_ref[rs, :] = u[:, C_LX:C_LX + LRU_WIDTH]
        lg_ref[rs, :] = u[:, C_LG:C_LG + LRU_WIDTH]

    starts = list(range(0, h_ref.shape[0], sub_rows))
    u_next = project(starts[0])
    for si, r0 in enumerate(starts):
        u = u_next
        if si + 1 < len(starts):
            u_next = project(starts[si + 1])
        head_groups(r0, u)


def _in_call(h, mods, row_map, l, wts, cos, sin, tm, sub_rows):
    n, d = h.shape
    tab_tiles = cos.shape[0] // tm
    row = lambda w: pl.BlockSpec((tm, w), lambda i: (i, 0))
    tab = pl.BlockSpec((tm, LANE), lambda i: (i % tab_tiles, 0))
    stacked = [wts["g_mix"], wts["w_all"], wts["w_qb"], wts["w_kvb"], wts["g_na"], wts["g_cq"], wts["g_ckv"],
               wts["g_mla"]]
    out_w =[(NA_WIDTH, BF), (NA_WIDTH, BF), (NA_WIDTH, BF), (MLA_QK_W, BF), (MLA_QK_W, BF),
             (MLA_HEADS * MLA_V, BF), (LRU_WIDTH, F32), (LRU_WIDTH, F32)]
    return pl.pallas_call(
        functools.partial(_in_kernel, sub_rows=sub_rows),
        grid=(n // tm,),
        in_specs=[row(d), _mod_spec(mods, l, 0, row_map), _mod_spec(mods, l, 1, row_map)]
                 + [_layer_spec(a, l) for a in stacked] + [tab, tab],
        out_specs=[row(w) for w, _ in out_w],
        out_shape=[jax.ShapeDtypeStruct((n, w), dt) for w, dt in out_w],
        compiler_params=_params("parallel"),
        name="in_proj",
    )(h, mods, mods, *stacked, cos, sin)


def _softmax_pv(s_a, s_b, v_a, v_b):
    m = jnp.maximum(jnp.max(s_a, axis=-1, keepdims=True), jnp.max(s_b, axis=-1, keepdims=True))
    p_a = jnp.exp2(s_a - m)
    p_b = jnp.exp2(s_b - m)
    l = jnp.sum(p_a, axis=-1, keepdims=True) + jnp.sum(p_b, axis=-1, keepdims=True)
    o = _dot(p_a.astype(BF), v_a) + _dot(p_b.astype(BF), v_b)
    return o / l


def _na_block_geometry(rows):
    nblk = rows // NA_QROWS
    half = NA_WIN_H // 2
    ks = np.clip(np.arange(nblk) * NA_QROWS - half, 0, rows - NA_KROWS)
    pats = []
    for j in range(nblk):
        r = j * NA_QROWS + np.arange(NA_QROWS)
        rs = np.clip(r - half, 0, rows - NA_WIN_H)
        assert ks[j] <= rs.min() and rs.max() + NA_WIN_H <= ks[j] + NA_KROWS
        pats.append(tuple(zip((r - ks[j]).tolist(), (rs - ks[j]).tolist())))
    cls = [0 if j == 0 else (2 if j == nblk - 1 else 1) for j in range(nblk)]
    by_cls = {}
    for j in range(nblk):
        assert by_cls.setdefault(cls[j], pats[j]) == pats[j]
    return [by_cls[c] for c in range(NA_CLASSES)]


def _na_kernel(q_ref, k_ref, v_ref, kc_ref, vc_ref, bias_ref, o_ref, *, rows):
    nblk = rows // NA_QROWS
    nq = NA_QROWS * GRID_W
    nk = NA_KROWS * GRID_W
    first_half = lax.broadcasted_iota(jnp.int32, (nq, LANE), 1) < NA_HEAD_DIM

    def block_geometry(sub):
        j = pl.program_id(1) * NA_BLOCKS_PER_STEP + sub
        ks = jnp.clip(j * NA_QROWS - NA_WIN_H // 2, 0, rows - NA_KROWS)
        cls = jnp.where(j == 0, 0, jnp.where(j == nblk - 1, 2, 1))
        return pl.multiple_of(ks * GRID_W, GRID_W), cls

    geom = [block_geometry(sub) for sub in range(NA_BLOCKS_PER_STEP)]

    def scores(sub, hd):
        start, cls = geom[sub]
        lo = (hd // 2) * LANE
        q2 = q_ref[sub * nq:(sub + 1) * nq, lo:lo + LANE]
        qm = jnp.where(first_half if hd % 2 == 0 else ~first_half, q2, jnp.zeros_like(q2))
        s_w = _dot_nt(qm, k_ref[pl.ds(start, nk), lo:lo + LANE]) + bias_ref[cls, hd]
        return s_w, _dot_nt(qm, kc_ref[:, lo:lo + LANE])

    work = [(sub, hd) for sub in range(NA_BLOCKS_PER_STEP) for hd in range(NA_HEADS)]
    nxt = scores(*work[0])
    outs = []
    for i, (sub, hd) in enumerate(work):
        s_w, s_c = nxt
        if i + 1 < len(work):
            nxt = scores(*work[i + 1])
        start, _ = geom[sub]
        lo = (hd // 2) * LANE
        outs.append(_softmax_pv(s_w, s_c, v_ref[pl.ds(start, nk), lo:lo + LANE], vc_ref[:, lo:lo + LANE]))
        if hd % 2 == 1:
            o_ref[sub * nq:(sub + 1) * nq, lo:lo + LANE] = jnp.where(first_half, outs[0], outs[1]).astype(BF)
            outs = []


def _na_call(q, k, v, kc, vc, bias, l, batch):
    n, w = q.shape
    s = n // batch
    c = kc.shape[0] // batch
    rows = s // GRID_W
    q3, k3, v3 = (a.reshape(batch, s, w) for a in (q, k, v))
    kc3, vc3 = (a.reshape(batch, c, w) for a in (kc, vc))
    nq = NA_BLOCKS_PER_STEP * NA_QROWS * GRID_W
    out = pl.pallas_call(
        functools.partial(_na_kernel, rows=rows),
        grid=(batch, rows // (NA_BLOCKS_PER_STEP * NA_QROWS)),
        in_specs=[
            pl.BlockSpec((None, nq, w), lambda b, j: (b, j, 0)),
            pl.BlockSpec((None, s, w), lambda b, j: (b, 0, 0)),
            pl.BlockSpec((None, s, w), lambda b, j: (b, 0, 0)),
            pl.BlockSpec((None, c, w), lambda b, j: (b, 0, 0)),
            pl.BlockSpec((None, c, w), lambda b, j: (b, 0, 0)),
            _layer_spec(bias, l),
        ],
        out_specs=pl.BlockSpec((None, nq, w), lambda b, j: (b, j, 0)),
        out_shape=jax.ShapeDtypeStruct((batch, s, w), BF),
        compiler_params=_params("parallel", "arbitrary"),
        name="nbr_attn",
    )(q3, k3, v3, kc3, vc3, bias)
    return out.reshape(n, w)


def _na_bias_tables(rpb, rows):
    depth = rpb.shape[0]
    qc = np.arange(GRID_W)
    kcol = np.arange(GRID_W)
    cs = np.clip(qc - NA_WIN_W // 2, 0, GRID_W - NA_WIN_W)
    ok = (kcol[None, :] >= cs[:, None]) & (kcol[None, :] < cs[:, None] + NA_WIN_W)
    col_rel = np.clip(kcol[None, :] - qc[:, None], 1 - NA_WIN_W, NA_WIN_W - 1) + NA_WIN_W - 1
    sel = (col_rel[None] == np.arange(2 * NA_WIN_W - 1)[:, None, None]).astype(np.float32)
    t = jnp.einsum("lhrd,dqk->lhrqk", rpb.astype(F32), jnp.asarray(sel), precision=lax.Precision.HIGHEST)
    t = jnp.where(ok, t * LOG2E, NEG_BIG)
    neg = jnp.full((depth, NA_HEADS, GRID_W, GRID_W), NEG_BIG, F32)
    classes = []
    for pat in _na_block_geometry(rows):
        qrows = []
        for qo, wo in pat:
            blocks = [t[:, :, kr - qo + NA_WIN_H - 1] if wo <= kr < wo + NA_WIN_H else neg for kr in range(NA_KROWS)]
            qrows.append(jnp.concatenate(blocks, axis=-1))
        classes.append(jnp.concatenate(qrows, axis=-2))
    return jnp.stack(classes, axis=1)


MLA_TILES_PER_STEP = 2


def _mla_kernel(q_ref, k_ref, v_ref, kc_ref, vc_ref, o_ref, *, tq):
    first_half = lax.broadcasted_iota(jnp.int32, (tq, LANE), 1) < MLA_V

    def scores(sub, hd):
        hl = hd * LANE
        qh = q_ref[sub * tq:(sub + 1) * tq, hl:hl + LANE]
        return _dot_nt(qh, k_ref[:, hl:hl + LANE]), _dot_nt(qh, kc_ref[:, hl:hl + LANE])

    work = [(sub, hd) for sub in range(q_ref.shape[0] // tq) for hd in range(MLA_HEADS)]
    nxt = scores(*work[0])
    outs = []
    for i, (sub, hd) in enumerate(work):
        s_l, s_c = nxt
        if i + 1 < len(work):
            nxt = scores(*work[i + 1])
        lo = (hd // 2) * LANE
        outs.append(_softmax_pv(s_l, s_c, v_ref[:, lo:lo + LANE], vc_ref[:, lo:lo + LANE]))
        if hd % 2 == 1:
            o_ref[sub * tq:(sub + 1) * tq, lo:lo + LANE] = jnp.where(first_half, outs[0], outs[1]).astype(BF)
            outs = []


def _mla_call(q, k, v, kc, vc, batch, tq):
    n, wq = q.shape
    wv = v.shape[1]
    s = n // batch
    c = kc.shape[0] // batch
    q3, k3 = q.reshape(batch, s, wq), k.reshape(batch, s, wq)
    v3 = v.reshape(batch, s, wv)
    kc3, vc3 = kc.reshape(batch, c, wq), vc.reshape(batch, c, wv)
    step_q = MLA_TILES_PER_STEP * tq
    out = pl.pallas_call(
        functools.partial(_mla_kernel, tq=tq),
        grid=(batch, s // step_q),
        in_specs=[
            pl.BlockSpec((None, step_q, wq), lambda b, i: (b, i, 0)),
            pl.BlockSpec((None, s, wq), lambda b, i: (b, 0, 0)),
            pl.BlockSpec((None, s, wv), lambda b, i: (b, 0, 0)),
            pl.BlockSpec((None, c, wq), lambda b, i: (b, 0, 0)),
            pl.BlockSpec((None, c, wv), lambda b, i: (b, 0, 0)),
        ],
        out_specs=pl.BlockSpec((None, step_q, wv), lambda b, i: (b, i, 0)),
        out_shape=jax.ShapeDtypeStruct((batch, s, wv), BF),
        compiler_params=_params("parallel", "arbitrary"),
        name="latent_attn",
    )(q3, k3, v3, kc3, vc3)
    return out.reshape(n, wv)


def _softmax_pv1(s, v):
    m = jnp.max(s, axis=-1, keepdims=True)
    p = jnp.exp2(s - m)
    l = jnp.sum(p, axis=-1, keepdims=True)
    return _dot(p.astype(BF), v) / l


def _ctx_attn_kernel(q_ref, k_ref, v_ref, mq_ref, mk_ref, mv_ref, ona_ref, omla_ref):
    c = q_ref.shape[0]
    first_half = lax.broadcasted_iota(jnp.int32, (c, LANE), 1) < NA_HEAD_DIM
    for p in range(NA_WIDTH // LANE):
        lo = p * LANE
        q2 = q_ref[:, lo:lo + LANE]
        k2 = k_ref[:, lo:lo + LANE]
        v2 = v_ref[:, lo:lo + LANE]
        outs = []
        for hh in range(2):
            qm = jnp.where(first_half if hh == 0 else ~first_half, q2, jnp.zeros_like(q2))
            outs.append(_softmax_pv1(_dot_nt(qm, k2), v2))
        ona_ref[:, lo:lo + LANE] = jnp.where(first_half, outs[0], outs[1]).astype(BF)
    for p in range(MLA_HEADS // 2):
        lo = p * LANE
        v2 = mv_ref[:, lo:lo + LANE]
        outs = []
        for hh in range(2):
            hl = (2 * p + hh) * LANE
            outs.append(_softmax_pv1(_dot_nt(mq_ref[:, hl:hl + LANE], mk_ref[:, hl:hl + LANE]), v2))
        omla_ref[:, lo:lo + LANE] = jnp.where(first_half, outs[0], outs[1]).astype(BF)


def _ctx_attn_call(q, k, v, mq, mk, mv, batch):
    nc = q.shape[0]
    c = nc // batch
    ins = [q, k, v, mq, mk, mv]
    ins3 = [a.reshape(batch, c, a.shape[1]) for a in ins]
    spec = lambda w: pl.BlockSpec((None, c, w), lambda b: (b, 0, 0))
    ona, omla = pl.pallas_call(
        _ctx_attn_kernel,
        grid=(batch,),
        in_specs=[spec(a.shape[1]) for a in ins],
        out_specs=[spec(NA_WIDTH), spec(MLA_HEADS * MLA_V)],
        out_shape=[jax.ShapeDtypeStruct((batch, c, NA_WIDTH), BF),
                   jax.ShapeDtypeStruct((batch, c, MLA_HEADS * MLA_V), BF)],
        compiler_params=_params("parallel"),
        name="ctx_attn",
    )(*ins3)
    return ona.reshape(nc, NA_WIDTH), omla.reshape(nc, MLA_HEADS * MLA_V)


LRU_PAD = SUBLANE
LRU_CHUNK = 512
LRU_SEGS = SUBLANE
LRU_SEG_GAP = 4
LRU_TILES = LRU_WIDTH // LANE
LRU_SCAN_UNROLL = 8


def _lru_pitch(n):
    return n // LRU_SEGS + LRU_SEG_GAP


def _lru_pieces(t0, cn, n):
    seg = n // LRU_SEGS
    pitch = _lru_pitch(n)
    out = []
    for k in range(t0 // seg, (t0 + cn - 1) // seg + 1):
        lo, hi = max(t0, k * seg), min(t0 + cn, (k + 1) * seg)
        out.append((lo - t0, hi - t0, k * pitch + lo - k * seg))
    return out


def _lru_coeffs(x_ref, n, xpad_ref, a_ref, u_ref, cw_ref, cb_ref, wg_ref, bg_ref, sp2):
    w = LRU_WIDTH
    zeros = jnp.zeros((LRU_PAD, w), F32)
    xpad_ref[0:LRU_PAD, :] = zeros
    xpad_ref[LRU_PAD:LRU_PAD + n, :] = x_ref[...]
    xpad_ref[LRU_PAD + n:2 * LRU_PAD + n, :] = zeros
    left = (LRU_CONV_W - 1) // 2
    for c0 in range(0, n, LRU_CHUNK):
        cn = min(LRU_CHUNK, n - c0)
        xc = jnp.zeros((cn, w), F32) + cb_ref[...]
        for j in range(LRU_CONV_W):
            off = LRU_PAD + c0 + j - left
            xc = xc + cw_ref[j:j + 1, :] * xpad_ref[off:off + cn, :]
        t = jnp.tanh(_dot(xc.astype(BF), wg_ref[...]) + bg_ref[...])
        xh = 0.5 * xc
        for dr in range(2):
            a = jnp.exp2(sp2[:, dr * w:(dr + 1) * w] * (1.0 + t[:, dr * w:(dr + 1) * w]))
            y = 1.0 - a * a
            root = jnp.where(y > 0.0, y * lax.rsqrt(y), 0.0)
            u = root * ((1.0 + t[:, (2 + dr) * w:(3 + dr) * w]) * xh)
            for lo, hi, row in _lru_pieces(c0, cn, n):
                for j in range(LRU_TILES):
                    a_ref[dr, j, row:row + hi - lo, :] = a[lo:hi, j * LANE:(j + 1) * LANE]
                    u_ref[dr, j, row:row + hi - lo, :] = u[lo:hi, j * LANE:(j + 1) * LANE]


def _lru_scan(n, a_ref, u_ref, hs_ref, h0):
    seg = n // LRU_SEGS
    pitch = _lru_pitch(n)
    chains = [(d, j) for d in range(2) for j in range(LRU_TILES)]

    def rows(d, t):
        return pl.ds(t if d == 0 else seg - 1 - t, LRU_SEGS, stride=pitch)

    def sweep_transfer(t, carry):
        out = []
        for (d, j), (f, p) in zip(chains, carry):
            a = a_ref[d, j, rows(d, t), :]
            out.append((a * f + u_ref[d, j, rows(d, t), :], a * p))
        return tuple(out)

    zeros = jnp.zeros((LRU_SEGS, LANE), F32)
    ones = jnp.ones((LRU_SEGS, LANE), F32)
    transfer = lax.fori_loop(0, seg, sweep_transfer, tuple((zeros, ones) for _ in chains), unroll=LRU_SCAN_UNROLL)

    entries, exits = [], []
    for (d, j), (f, p) in zip(chains, transfer):
        state = h0[d][j]
        entry = [None] * LRU_SEGS
        for k in (range(LRU_SEGS) if d == 0 else range(LRU_SEGS - 1, -1, -1)):
            entry[k] = state
            state = p[k:k + 1, :] * state + f[k:k + 1, :]
        entries.append(jnp.concatenate(entry, axis=0))
        exits.append(state)

    def sweep_states(t, carry):
        out = []
        for (d, j), h in zip(chains, carry):
            h = a_ref[d, j, rows(d, t), :] * h + u_ref[d, j, rows(d, t), :]
            hs_ref[d, j, rows(d, t), :] = h
            out.append(h)
        return tuple(out)

    lax.fori_loop(0, seg, sweep_states, tuple(entries), unroll=LRU_SCAN_UNROLL)
    return [[exits[d * LRU_TILES + j] for j in range(LRU_TILES)] for d in range(2)]


def _lru_output(n, hs_ref, lg_ref, o_ref):
    for c0 in range(0, n, LRU_CHUNK):
        cn = min(LRU_CHUNK, n - c0)
        parts = []
        for lo, hi, row in _lru_pieces(c0, cn, n):
            parts.append(jnp.concatenate(
                [hs_ref[0, j, row:row + hi - lo, :] + hs_ref[1, j, row:row + hi - lo, :] for j in range(LRU_TILES)],
                axis=-1))
        hsum = parts[0] if len(parts) == 1 else jnp.concatenate(parts, axis=0)
        o_ref[c0:c0 + cn, :] = (_gelu_tanh(lg_ref[c0:c0 + cn, :]) * hsum).astype(BF)


def _lru_kernel(lx_ref, lg_ref, clx_ref, clg_ref, cw_ref, cb_ref, wg_ref, bg_ref, lam_ref,
                o_ref, oc_ref, xpad_ref, a_ref, u_ref, hs_ref, ac_ref, uc_ref):
    n = lx_ref.shape[0]
    nc = clx_ref.shape[0]
    nlam = -lam_ref[...]
    sp = jnp.maximum(nlam, 0.0) + jnp.log(1.0 + jnp.exp(-jnp.abs(nlam)))
    sp2 = (-0.5 * LRU_C * LOG2E) * sp
    zero = jnp.zeros((1, LANE), F32)
    _lru_coeffs(clx_ref, nc, xpad_ref, ac_ref, uc_ref, cw_ref, cb_ref, wg_ref, bg_ref, sp2)
    h_ctx = _lru_scan(nc, ac_ref, uc_ref, hs_ref, [[zero] * LRU_TILES] * 2)
    _lru_output(nc, hs_ref, clg_ref, oc_ref)
    _lru_coeffs(lx_ref, n, xpad_ref, a_ref, u_ref, cw_ref, cb_ref, wg_ref, bg_ref, sp2)
    _lru_scan(n, a_ref, u_ref, hs_ref, h_ctx)
    _lru_output(n, hs_ref, lg_ref, o_ref)


def _lru_call(lx, lg, clx, clg, l, wts, batch):
    n, w = lx.shape
    s = n // batch
    c = clx.shape[0] // batch
    seq = lambda a, ln: a.reshape(batch, ln, w)
    lat = pl.BlockSpec((None, s, w), lambda b: (b, 0, 0))
    ctx = pl.BlockSpec((None, c, w), lambda b: (b, 0, 0))
    stacked = [wts["conv_w"], wts["conv_b"], wts["w_gate"], wts["b_gate"], wts["lam"]]
    o, oc = pl.pallas_call(
        _lru_kernel,
        grid=(batch,),
        in_specs=[lat, lat, ctx, ctx] + [_layer_spec(a, l) for a in stacked],
        out_specs=[lat, ctx],
        out_shape=[jax.ShapeDtypeStruct((batch, s, w), BF), jax.ShapeDtypeStruct((batch, c, w), BF)],
        scratch_shapes=[pltpu.VMEM((s + 2 * LRU_PAD, w), F32)]
                       + [pltpu.VMEM((2, LRU_TILES, LRU_SEGS * _lru_pitch(ln), LANE), F32) for ln in (s, s, s, c, c)],
        compiler_params=_params("parallel"),
        name="rglru",
    )(seq(lx, s), seq(lg, s), seq(clx, c), seq(clg, c), *stacked)
    return o.reshape(n, w), oc.reshape(batch * c, w)


def _out_kernel(h_ref, na_ref, mla_ref, lru_ref, gm_ref, shf_ref, scf_ref, gf_ref, g_ref,
                wo_ref, wg_ref, wu_ref, wd_ref, o_ref, *, hid_chunk, sub_rows):
    hidden = wg_ref.shape[1]

    def residual_and_norm(r0):
        rs = slice(r0, r0 + sub_rows)
        o_cat = jnp.concatenate([na_ref[rs, :], mla_ref[rs, :], lru_ref[rs, :]], axis=-1)
        h1 = h_ref[rs, :] + gm_ref[...] * _dot(o_cat, wo_ref[...])
        y = h1 * lax.rsqrt(jnp.mean(h1 * h1, axis=-1, keepdims=True) + EPS) * g_ref[...]
        return h1, (y * (1.0 + scf_ref[...]) + shf_ref[...]).astype(BF)

    def gate_up(y, c0):
        return _dot(y, wg_ref[:, c0:c0 + hid_chunk]), _dot(y, wu_ref[:, c0:c0 + hid_chunk])

    starts = list(range(0, h_ref.shape[0], sub_rows))
    h1, y = residual_and_norm(starts[0])
    nxt = gate_up(y, 0)
    for si, r0 in enumerate(starts):
        acc = None
        h1_next = y_next = None
        for c0 in range(0, hidden, hid_chunk):
            gate, up = nxt
            if c0 + hid_chunk < hidden:
                nxt = gate_up(y, c0 + hid_chunk)
            elif si + 1 < len(starts):
                h1_next, y_next = residual_and_norm(starts[si + 1])
                nxt = gate_up(y_next, 0)
            d = _dot((_silu(gate) * up).astype(BF), wd_ref[c0:c0 + hid_chunk, :])
            acc = d if acc is None else acc + d
        o_ref[r0:r0 + sub_rows, :] = h1 + gf_ref[...] * acc
        h1, y = h1_next, y_next


def _out_call(h, na_o, mla_o, lru_o, mods, row_map, l, wts, tm, sub_rows):
    n, d = h.shape
    row = lambda w: pl.BlockSpec((tm, w), lambda i: (i, 0))
    stacked = [wts["g_ffn"], wts["w_out"], wts["ffn_gate"], wts["ffn_up"], wts["ffn_down"]]
    hidden = wts["ffn_gate"].shape[-1]
    hid_chunk = 2 * LANE if hidden % (2 * LANE) == 0 else hidden
    return pl.pallas_call(
        functools.partial(_out_kernel, hid_chunk=hid_chunk, sub_rows=sub_rows),
        grid=(n // tm,),
        in_specs=[row(d), row(na_o.shape[1]), row(mla_o.shape[1]), row(lru_o.shape[1])]
                 + [_mod_spec(mods, l, which, row_map) for which in (2, 3, 4, 5)]
                 + [_layer_spec(a, l) for a in stacked],
        out_specs=row(d),
        out_shape=jax.ShapeDtypeStruct((n, d), F32),
        compiler_params=_params("parallel"),
        name="out_ffn",
    )(h, na_o, mla_o, lru_o, mods, mods, mods, mods, *stacked)


def _rope_perm():
    j = np.arange(MLA_ROPE)
    first = (j % (MLA_ROPE // 2)) < MLA_ROPE // 4
    partner = np.where(first, j + MLA_ROPE // 4, j - MLA_ROPE // 4)
    sign = np.where(first, -1.0, 1.0).astype(np.float32)
    return partner, sign


def _rope_tables(s):
    t = np.arange(s)
    row = (t // GRID_W).astype(np.float32)
    col = (t % GRID_W).astype(np.float32)
    n_freq = MLA_ROPE // 4
    inv = jnp.asarray(ROPE_BASE, F32) ** (-jnp.arange(n_freq, dtype=F32) / n_freq)
    ar = jnp.asarray(row)[:, None] * inv
    ac = jnp.asarray(col)[:, None] * inv
    ang = jnp.concatenate([ar, ar, ac, ac], axis=-1)
    pad = LANE - MLA_NOPE - MLA_ROPE
    cos = jnp.concatenate([jnp.ones((s, MLA_NOPE), F32), jnp.cos(ang), jnp.zeros((s, pad), F32)], axis=-1)
    sin = jnp.concatenate([jnp.zeros((s, MLA_NOPE), F32), jnp.sin(ang), jnp.zeros((s, pad), F32)], axis=-1)
    return cos, sin


def _lane_tile(nope, rope):
    lead = nope.shape[:-1]
    pad = jnp.zeros(lead + (LANE - MLA_NOPE - MLA_ROPE,), nope.dtype)
    return jnp.concatenate([nope, rope, pad], axis=-1)


def _prep_weights(p):
    partner, sign = _rope_perm()
    w_in = p["w_in"]
    depth, d, _ = w_in.shape
    cols = np.cumsum((NA_WIDTH, NA_WIDTH, NA_WIDTH, MLA_Q_RANK, MLA_KV_RANK, MLA_ROPE, LRU_WIDTH, LRU_WIDTH))
    w_kr = w_in[..., cols[4]:cols[5]]
    zn = jnp.zeros((depth, d, MLA_NOPE), F32)
    w_all = jnp.concatenate([
        w_in[..., :cols[4]], _lane_tile(zn, w_kr), _lane_tile(zn, w_kr[..., partner]), w_in[..., cols[5]:]], axis=-1)
    assert w_all.shape[-1] == W_ALL

    wq = p["mla_w_qb"].reshape(depth, MLA_Q_RANK, MLA_HEADS, MLA_NOPE + MLA_ROPE)
    wq_n, wq_r = wq[..., :MLA_NOPE], wq[..., MLA_NOPE:]
    w_qb = jnp.concatenate([
        _lane_tile(wq_n, wq_r).reshape(depth, MLA_Q_RANK, MLA_QK_W),
        _lane_tile(jnp.zeros_like(wq_n), wq_r[..., partner]).reshape(depth, MLA_Q_RANK, MLA_QK_W)], axis=-1)

    wkv = p["mla_w_kvb"].reshape(depth, MLA_KV_RANK, MLA_HEADS, MLA_NOPE + MLA_V)
    wk_n, wv = wkv[..., :MLA_NOPE], wkv[..., MLA_NOPE:]
    w_kvb = jnp.concatenate([
        _lane_tile(wk_n, jnp.zeros(wk_n.shape[:-1] + (MLA_ROPE,), F32)).reshape(depth, MLA_KV_RANK, MLA_QK_W),
        wv.reshape(depth, MLA_KV_RANK, MLA_HEADS * MLA_V)], axis=-1)

    qg, kg = p["mla_q_g"], p["mla_k_g"]
    z_n, z_r = jnp.zeros((depth, MLA_NOPE), F32), jnp.zeros((depth, MLA_ROPE), F32)
    sgn = jnp.asarray(sign)
    zrow = jnp.zeros((depth, LANE), F32)
    g_mla = jnp.stack([
        _lane_tile(qg[:, :MLA_NOPE], qg[:, MLA_NOPE:]) * MLA_SCALE,
        _lane_tile(z_n, sgn * qg[:, MLA_NOPE:][:, partner]) * MLA_SCALE,
        _lane_tile(kg[:, :MLA_NOPE], z_r),
        _lane_tile(z_n, kg[:, MLA_NOPE:]),
        _lane_tile(z_n, sgn * kg[:, MLA_NOPE:][:, partner]),
        zrow, zrow, zrow], axis=1)

    g_na = jnp.concatenate([jnp.tile(p["na_q_g"], (1, NA_HEADS)) * NA_SCALE, jnp.tile(p["na_k_g"], (1, NA_HEADS))],
                           axis=-1)[:, None]

    wa, wx = p["lru_w_a"], p["lru_w_x"]
    wg4 = jnp.stack([wa[:, 0], wa[:, 1], wx[:, 0], wx[:, 1]], axis=1)
    eye = jnp.eye(LRU_HEADS, dtype=F32)
    w_gate = (wg4[:, :, :, :, None, :] * eye[None, None, :, None, :, None])
    w_gate = w_gate.transpose(0, 2, 3, 1, 4, 5).reshape(depth, LRU_WIDTH, 4 * LRU_WIDTH)
    ba, bx = p["lru_b_a"], p["lru_b_x"]
    b_gate = jnp.concatenate([ba[:, 0], ba[:, 1], bx[:, 0], bx[:, 1]], axis=-1)[:, None]

    return {
        "g_mix": p["norm_mix_g"][:, None], "g_ffn": p["norm_ffn_g"][:, None],
        "w_all": w_all.astype(BF), "w_qb": w_qb.astype(BF), "w_kvb": w_kvb.astype(BF),
        "g_na": g_na, "g_cq": p["mla_cq_g"][:, None], "g_ckv": p["mla_ckv_g"][:, None], "g_mla": g_mla,
        "conv_w": p["lru_conv_w"], "conv_b": p["lru_conv_b"][:, None],
        "w_gate": (0.5 * w_gate).astype(BF), "b_gate": 0.5 * b_gate, "lam": p["lru_lambda"].reshape(depth, 1, 2 * LRU_WIDTH),
        "w_out": p["w_out"].astype(BF), "ffn_gate": p["ffn_w_gate"].astype(BF),
        "ffn_up": p["ffn_w_up"].astype(BF), "ffn_down": p["ffn_w_down"].astype(BF),
    }


def kernel(x, c, ctx, c_ctx, ada_w, ada_b, norm_mix_g, norm_ffn_g, w_in, na_q_g, na_k_g, na_rpb, mla_cq_g, mla_w_qb, mla_ckv_g, mla_w_kvb, mla_q_g, mla_k_g, lru_conv_w, lru_conv_b, lru_w_a, lru_b_a, lru_w_x, lru_b_x, lru_lambda, w_out, ffn_w_gate, ffn_w_up, ffn_w_down):
    batch, s, d = x.shape
    cl = ctx.shape[1]
    depth = ada_w.shape[0]
    rows = s // GRID_W
    assert s % GRID_W == 0 and rows % (NA_BLOCKS_PER_STEP * NA_QROWS) == 0 and rows >= 2 * NA_KROWS
    p = dict(norm_mix_g=norm_mix_g, norm_ffn_g=norm_ffn_g, w_in=w_in, na_q_g=na_q_g, na_k_g=na_k_g,
             mla_cq_g=mla_cq_g, mla_w_qb=mla_w_qb, mla_ckv_g=mla_ckv_g, mla_w_kvb=mla_w_kvb, mla_q_g=mla_q_g,
             mla_k_g=mla_k_g, lru_conv_w=lru_conv_w, lru_conv_b=lru_conv_b, lru_w_a=lru_w_a, lru_b_a=lru_b_a,
             lru_w_x=lru_w_x, lru_b_x=lru_b_x, lru_lambda=lru_lambda, w_out=w_out, ffn_w_gate=ffn_w_gate,
             ffn_w_up=ffn_w_up, ffn_w_down=ffn_w_down)

    sub = 512 if s % 512 == 0 else GRID_W * NA_QROWS
    tm = 2 * sub if s % (2 * sub) == 0 else sub
    subc = sub if (batch * cl) % sub == 0 else cl
    tmc = 2 * subc if (batch * cl) % (2 * subc) == 0 else subc
    tq = 512

    nmod = -(-(batch + 1) // SUBLANE) * SUBLANE
    cvec = jnp.concatenate([c, c_ctx[None], jnp.zeros((nmod - batch - 1, d), F32)], axis=0)
    mods = _ada_call(cvec, ada_w, ada_b).reshape(depth, nmod, 6, 1, d)
    tiles_per_batch = s // tm
    lat_row = lambda i: i // tiles_per_batch
    ctx_row = lambda i: batch

    wts = _prep_weights(p)
    na_bias = _na_bias_tables(na_rpb, rows)
    cos, sin = _rope_tables(s)
    cos_c = jnp.concatenate([jnp.ones((tmc, MLA_NOPE + MLA_ROPE), F32),
                             jnp.zeros((tmc, LANE - MLA_NOPE - MLA_ROPE), F32)], axis=-1)
    sin_c = jnp.zeros((tmc, LANE), F32)

    h = x.reshape(batch * s, d)
    hc = ctx.reshape(batch * cl, d)
    for l in range(depth):
        last = l == depth - 1
        q, k, v, mq, mk, mv, lx, lg = _in_call(h, mods, lat_row, l, wts, cos, sin, tm, sub // 2)
        cq_, ck, cv, cmq, cmk, cmv, clx, clg = _in_call(hc, mods, ctx_row, l, wts, cos_c, sin_c, tmc, subc // 2)

        na_o = _na_call(q, k, v, ck, cv, na_bias, l, batch)
        mla_o = _mla_call(mq, mk, mv, cmk, cmv, batch, tq)
        lru_o, lru_c = _lru_call(lx, lg, clx, clg, l, wts, batch)
        h = _out_call(h, na_o, mla_o, lru_o, mods, lat_row, l, wts, tm, sub)

        if not last:
            na_c, mla_c = _ctx_attn_call(cq_, ck, cv, cmq, cmk, cmv, batch)
            hc = _out_call(hc, na_c, mla_c, lru_c, mods, ctx_row, l, wts, tmc, subc)
    return h.reshape(batch, s, d)
```

```python
import functools
import math

import numpy as np
import jax
import jax.numpy as jnp
from jax import lax
from jax.experimental import pallas as pl
from jax.experimental.pallas import tpu as pltpu

GRID_W = 64
NA_HEADS = 6
NA_HEAD_DIM = 64
NA_WIDTH = NA_HEADS * NA_HEAD_DIM
NA_WIN_H = 8
NA_WIN_W = 16
MLA_HEADS = 6
MLA_Q_RANK = 256
MLA_KV_RANK = 128
MLA_NOPE = 64
MLA_ROPE = 32
MLA_V = 64
LRU_WIDTH = 256
LRU_HEADS = 4
LRU_BLOCK = LRU_WIDTH // LRU_HEADS
LRU_CONV_W = 4
LRU_C = 8.0
ROPE_BASE = 10000.0
EPS = 1e-6
LOG2E = math.log2(math.e)
NA_SCALE = NA_HEAD_DIM ** -0.5 * LOG2E
MLA_SCALE = (MLA_NOPE + MLA_ROPE) ** -0.5 * LOG2E

LANE = 128
SUBLANE = 8
MLA_QK_W = MLA_HEADS * LANE
NEG_BIG = -1e30
VMEM_LIMIT = 56 * 1024 * 1024

NA_QROWS = 4
NA_KROWS = NA_QROWS + NA_WIN_H
NA_CLASSES = 3
NA_BLOCKS_PER_STEP = 4

C_Q, C_K, C_V = 0, NA_WIDTH, 2 * NA_WIDTH
C_CQ = 3 * NA_WIDTH
C_CKV = C_CQ + MLA_Q_RANK
C_KR = C_CKV + MLA_KV_RANK
C_KRP = C_KR + LANE
C_LX = C_KRP + LANE
C_LG = C_LX + LRU_WIDTH
W_ALL = C_LG + LRU_WIDTH

BF = jnp.bfloat16
F32 = jnp.float32


def _dot(a, b):
    return jnp.dot(a, b, preferred_element_type=F32)


def _dot_nt(a, b):
    return lax.dot_general(a, b, (((1,), (1,)), ((), ())), preferred_element_type=F32)


def _silu(x):
    hx = 0.5 * x
    return hx + hx * jnp.tanh(hx)


def _gelu_tanh(x):
    return 0.5 * x * (1.0 + jnp.tanh(0.7978845608028654 * (x + 0.044715 * (x * x * x))))


def _params(*sem):
    return pltpu.CompilerParams(dimension_semantics=sem, vmem_limit_bytes=VMEM_LIMIT)


def _const_spec(shape):
    nd = len(shape)
    return pl.BlockSpec(shape, lambda *_: (0,) * nd, pipeline_mode=pl.Buffered(1))


def _layer_spec(arr, l):
    nd = arr.ndim
    return pl.BlockSpec((None,) + arr.shape[1:], lambda *_: (l,) + (0,) * (nd - 1), pipeline_mode=pl.Buffered(1))


def _mod_spec(mods, l, which, row_map):
    d = mods.shape[-1]
    return pl.BlockSpec((None, None, None, 1, d), lambda i: (l, row_map(i), which, 0, 0))


def _ada_kernel(c_ref, w_ref, b_ref, o_ref):
    s = _silu(c_ref[...]).astype(BF)
    o_ref[...] = _dot(s, w_ref[...].astype(BF)) + b_ref[...]


def _ada_call(cvec, ada_w, ada_b):
    depth, d, d6 = ada_w.shape
    r = cvec.shape[0]
    tn = 1024
    return pl.pallas_call(
        _ada_kernel,
        grid=(depth, d6 // tn),
        in_specs=[
            pl.BlockSpec((r, d), lambda l, j: (0, 0)),
            pl.BlockSpec((None, d, tn), lambda l, j: (l, 0, j)),
            pl.BlockSpec((None, 1, tn), lambda l, j: (l, 0, j)),
        ],
        out_specs=pl.BlockSpec((None, r, tn), lambda l, j: (l, 0, j)),
        out_shape=jax.ShapeDtypeStruct((depth, r, d6), F32),
        compiler_params=_params("parallel", "parallel"),
        name="adaln",
    )(cvec, ada_w, ada_b.reshape(depth, 1, d6))


def _in_kernel(h_ref, sh_ref, sc_ref, g_ref, wall_ref, wqb_ref, wkvb_ref, gna_ref, gcq_ref, gckv_ref,
               gm_ref, cos_ref, sin_ref,
               q_ref, k_ref, v_ref, mq_ref, mk_ref, mv_ref, lx_ref, lg_ref, *, sub_rows):
    lane = lax.broadcasted_iota(jnp.int32, (sub_rows, LANE), 1)
    na_first = lane < NA_HEAD_DIM
    nope = lane < MLA_NOPE

    def seg_rsqrt(t, in_first, n_first, n_second):
        t2 = t * t
        s1 = jnp.sum(jnp.where(in_first, t2, 0.0), axis=-1, keepdims=True) * (1.0 / n_first)
        s2 = jnp.sum(jnp.where(in_first, 0.0, t2), axis=-1, keepdims=True) * (1.0 / n_second)
        return lax.rsqrt(jnp.where(in_first, s1, s2) + EPS)

    def store_values(ref, rs, vals):
        for p in range(vals.shape[1] // LANE):
            vp = vals[:, p * LANE:(p + 1) * LANE]
            ref[rs, (2 * p) * LANE:(2 * p + 1) * LANE] = jnp.where(na_first, vp, 1.0).astype(BF)
            ref[rs, (2 * p + 1) * LANE:(2 * p + 2) * LANE] = jnp.where(na_first, 1.0, vp).astype(BF)

    def project(r0):
        x = h_ref[r0:r0 + sub_rows, :]
        y = x * lax.rsqrt(jnp.mean(x * x, axis=-1, keepdims=True) + EPS) * g_ref[...]
        y = y * (1.0 + sc_ref[...]) + sh_ref[...]
        return _dot(y.astype(BF), wall_ref[...])

    def head_groups(r0, u):
        rs = slice(r0, r0 + sub_rows)
        for p in range(2 * NA_WIDTH // LANE):
            lo = p * LANE
            t = u[:, C_Q + lo:C_Q + lo + LANE]
            tn = (t * seg_rsqrt(t, na_first, NA_HEAD_DIM, NA_HEAD_DIM) * gna_ref[:, lo:lo + LANE]).astype(BF)
            if lo < NA_WIDTH:
                q_ref[rs, lo:lo + LANE] = tn
            else:
                k_ref[rs, lo - NA_WIDTH:lo - NA_WIDTH + LANE] = tn
        store_values(v_ref, rs, u[:, C_V:C_V + NA_WIDTH])

        cos = cos_ref[rs, :]
        sin = sin_ref[rs, :]
        cq = u[:, C_CQ:C_CQ + MLA_Q_RANK]
        cqn = cq * lax.rsqrt(jnp.mean(cq * cq, axis=-1, keepdims=True) + EPS) * gcq_ref[...]
        qq = _dot(cqn.astype(BF), wqb_ref[...])
        a_q = gm_ref[0:1, :] * cos
        b_q = gm_ref[1:2, :] * sin
        for hd in range(MLA_HEADS):
            lo = hd * LANE
            qr = qq[:, lo:lo + LANE]
            qp = qq[:, MLA_QK_W + lo:MLA_QK_W + lo + LANE]
            mq_ref[rs, lo:lo + LANE] = (seg_rsqrt(qr, nope, MLA_NOPE, MLA_ROPE) * (qr * a_q + qp * b_q)).astype(BF)

        ckv = u[:, C_CKV:C_CKV + MLA_KV_RANK]
        ckvn = ckv * lax.rsqrt(jnp.mean(ckv * ckv, axis=-1, keepdims=True) + EPS) * gckv_ref[...]
        kv = _dot(ckvn.astype(BF), wkvb_ref[...])
        krr = u[:, C_KR:C_KR + LANE]
        krp = u[:, C_KRP:C_KRP + LANE]
        msr = jnp.sum(krr * krr, axis=-1, keepdims=True) * (1.0 / MLA_ROPE)
        kr = lax.rsqrt(msr + EPS) * (krr * (gm_ref[3:4, :] * cos) + krp * (gm_ref[4:5, :] * sin))
        for hd in range(MLA_HEADS):
            lo = hd * LANE
            kn = kv[:, lo:lo + LANE]
            ms = jnp.sum(kn * kn, axis=-1, keepdims=True) * (1.0 / MLA_NOPE)
            mk_ref[rs, lo:lo + LANE] = (kn * lax.rsqrt(ms + EPS) * gm_ref[2:3, :] + kr).astype(BF)
        store_values(mv_ref, rs, kv[:, MLA_QK_W:])

        lx_ref[rs, :] = u[:, C_LX:C_LX + LRU_WIDTH]
        lg_ref[rs, :] = u[:, C_LG:C_LG + LRU_WIDTH]

    starts = list(range(0, h_ref.shape[0], sub_rows))
    u_next = project(starts[0])
    for si, r0 in enumerate(starts):
        u = u_next
        if si + 1 < len(starts):
            u_next = project(starts[si + 1])
        head_groups(r0, u)


def _in_call(h, mods, row_map, l, wts, cos, sin, tm, sub_rows):
    n, d = h.shape
    tab_tiles = cos.shape[0] // tm
    row = lambda w: pl.BlockSpec((tm, w), lambda i: (i, 0))
    tab = pl.BlockSpec((tm, LANE), lambda i: (i % tab_tiles, 0))
    stacked = [wts["g_mix"], wts["w_all"], wts["w_qb"], wts["w_kvb"], wts["g_na"], wts["g_cq"], wts["g_ckv"],
               wts["g_mla"]]
    out_w = [(NA_WIDTH, BF), (NA_WIDTH, BF), (NA_HEADS * LANE, BF), (MLA_QK_W, BF), (MLA_QK_W, BF),
             (MLA_HEADS * LANE, BF), (LRU_WIDTH, F32), (LRU_WIDTH, F32)]
    return pl.pallas_call(
        functools.partial(_in_kernel, sub_rows=sub_rows),
        grid=(n // tm,),
        in_specs=[row(d), _mod_spec(mods, l, 0, row_map), _mod_spec(mods, l, 1, row_map)]
                 + [_layer_spec(a, l) for a in stacked] + [tab, tab],
        out_specs=[row(w) for w, _ in out_w],
        out_shape=[jax.ShapeDtypeStruct((n, w), dt) for w, dt in out_w],
        compiler_params=_params("parallel"),
        name="in_proj",
    )(h, mods, mods, *stacked, cos, sin)


def _attention_pipeline(work, scores, values, emit):
    s_next = scores(work[0])
    for i, item in enumerate(work):
        s_cur = s_next
        if i + 1 < len(work):
            s_next = scores(work[i + 1])
        m = functools.reduce(jnp.maximum, [jnp.max(s, axis=-1, keepdims=True) for s in s_cur])
        o = sum(_dot(jnp.exp2(s - m).astype(BF), v) for s, v in zip(s_cur, values(item)))
        emit(item, o / pltpu.roll(o, LANE // 2, 1))


def _na_block_geometry(rows):
    nblk = rows // NA_QROWS
    half = NA_WIN_H // 2
    ks = np.clip(np.arange(nblk) * NA_QROWS - half, 0, rows - NA_KROWS)
    pats = []
    for j in range(nblk):
        r = j * NA_QROWS + np.arange(NA_QROWS)
        rs = np.clip(r - half, 0, rows - NA_WIN_H)
        assert ks[j] <= rs.min() and rs.max() + NA_WIN_H <= ks[j] + NA_KROWS
        pats.append(tuple(zip((r - ks[j]).tolist(), (rs - ks[j]).tolist())))
    cls = [0 if j == 0 else (2 if j == nblk - 1 else 1) for j in range(nblk)]
    by_cls = {}
    for j in range(nblk):
        assert by_cls.setdefault(cls[j], pats[j]) == pats[j]
    return [by_cls[c] for c in range(NA_CLASSES)]


def _na_kernel(q_ref, k_ref, v_ref, kc_ref, vc_ref, bias_ref, o_ref, *, rows):
    nblk = rows // NA_QROWS
    nq = NA_QROWS * GRID_W
    nk = NA_KROWS * GRID_W
    first_half = lax.broadcasted_iota(jnp.int32, (nq, LANE), 1) < NA_HEAD_DIM

    def block_geometry(sub):
        j = pl.program_id(1) * NA_BLOCKS_PER_STEP + sub
        ks = jnp.clip(j * NA_QROWS - NA_WIN_H // 2, 0, rows - NA_KROWS)
        cls = jnp.where(j == 0, 0, jnp.where(j == nblk - 1, 2, 1))
        return pl.multiple_of(ks * GRID_W, GRID_W), cls

    geom = [block_geometry(sub) for sub in range(NA_BLOCKS_PER_STEP)]

    def scores(item):
        sub, hd = item
        start, cls = geom[sub]
        lo = (hd // 2) * LANE
        q2 = q_ref[sub * nq:(sub + 1) * nq, lo:lo + LANE]
        qm = jnp.where(first_half if hd % 2 == 0 else ~first_half, q2, jnp.zeros_like(q2))
        s_w = _dot_nt(qm, k_ref[pl.ds(start, nk), lo:lo + LANE]) + bias_ref[cls, hd]
        return s_w, _dot_nt(qm, kc_ref[:, lo:lo + LANE])

    def values(item):
        sub, hd = item
        return v_ref[pl.ds(geom[sub][0], nk), hd * LANE:(hd + 1) * LANE], vc_ref[:, hd * LANE:(hd + 1) * LANE]

    even_head = {}

    def emit(item, out):
        sub, hd = item
        if hd % 2 == 0:
            even_head[sub] = out
        else:
            lo = (hd // 2) * LANE
            o_ref[sub * nq:(sub + 1) * nq, lo:lo + LANE] = jnp.where(first_half, even_head[sub], out).astype(BF)

    _attention_pipeline([(sub, hd) for sub in range(NA_BLOCKS_PER_STEP) for hd in range(NA_HEADS)],
                        scores, values, emit)


def _na_call(q, k, v, kc, vc, bias, l, batch):
    n, w = q.shape
    s = n // batch
    c = kc.shape[0] // batch
    rows = s // GRID_W
    wv = v.shape[1]
    q3, k3, v3 = q.reshape(batch, s, w), k.reshape(batch, s, w), v.reshape(batch, s, wv)
    kc3, vc3 = kc.reshape(batch, c, w), vc.reshape(batch, c, wv)
    nq = NA_BLOCKS_PER_STEP * NA_QROWS * GRID_W
    out = pl.pallas_call(
        functools.partial(_na_kernel, rows=rows),
        grid=(batch, rows // (NA_BLOCKS_PER_STEP * NA_QROWS)),
        in_specs=[
            pl.BlockSpec((None, nq, w), lambda b, j: (b, j, 0)),
            pl.BlockSpec((None, s, w), lambda b, j: (b, 0, 0)),
            pl.BlockSpec((None, s, wv), lambda b, j: (b, 0, 0)),
            pl.BlockSpec((None, c, w), lambda b, j: (b, 0, 0)),
            pl.BlockSpec((None, c, wv), lambda b, j: (b, 0, 0)),
            _layer_spec(bias, l),
        ],
        out_specs=pl.BlockSpec((None, nq, w), lambda b, j: (b, j, 0)),
        out_shape=jax.ShapeDtypeStruct((batch, s, w), BF),
        compiler_params=_params("parallel", "arbitrary"),
        name="nbr_attn",
    )(q3, k3, v3, kc3, vc3, bias)
    return out.reshape(n, w)


def _na_bias_tables(rpb, rows):
    depth = rpb.shape[0]
    qc = np.arange(GRID_W)
    kcol = np.arange(GRID_W)
    cs = np.clip(qc - NA_WIN_W // 2, 0, GRID_W - NA_WIN_W)
    ok = (kcol[None, :] >= cs[:, None]) & (kcol[None, :] < cs[:, None] + NA_WIN_W)
    col_rel = np.clip(kcol[None, :] - qc[:, None], 1 - NA_WIN_W, NA_WIN_W - 1) + NA_WIN_W - 1
    n_dr, n_dc = 2 * NA_WIN_H - 1, 2 * NA_WIN_W - 1
    col_sel = (col_rel[None] == np.arange(n_dc)[:, None, None]).astype(np.float32)
    row_sel = np.zeros((NA_CLASSES, NA_QROWS, NA_KROWS, n_dr), np.float32)
    for c, pat in enumerate(_na_block_geometry(rows)):
        for i, (qo, wo) in enumerate(pat):
            for kr in range(wo, wo + NA_WIN_H):
                row_sel[c, i, kr, kr - qo + NA_WIN_H - 1] = 1.0
    in_window = (row_sel.sum(-1) > 0)[:, :, None, :, None] & ok[None, None, :, None, :]
    t = jnp.einsum("cird,lhde,eqk->lchiqrk", jnp.asarray(row_sel), rpb.astype(F32), jnp.asarray(col_sel),
                   precision=lax.Precision.HIGHEST)
    t = jnp.where(in_window[None, :, None], t * LOG2E, NEG_BIG)
    return t.reshape(depth, NA_CLASSES, NA_HEADS, NA_QROWS * GRID_W, NA_KROWS * GRID_W)


MLA_TILES_PER_STEP = 2


def _mla_kernel(q_ref, k_ref, v_ref, kc_ref, vc_ref, o_ref, *, tq):
    first_half = lax.broadcasted_iota(jnp.int32, (tq, LANE), 1) < MLA_V

    def scores(item):
        sub, hd = item
        hl = hd * LANE
        qh = q_ref[sub * tq:(sub + 1) * tq, hl:hl + LANE]
        return _dot_nt(qh, k_ref[:, hl:hl + LANE]), _dot_nt(qh, kc_ref[:, hl:hl + LANE])

    def values(item):
        hl = item[1] * LANE
        return v_ref[:, hl:hl + LANE], vc_ref[:, hl:hl + LANE]

    even_head = {}

    def emit(item, out):
        sub, hd = item
        if hd % 2 == 0:
            even_head[sub] = out
        else:
            lo = (hd // 2) * LANE
            o_ref[sub * tq:(sub + 1) * tq, lo:lo + LANE] = jnp.where(first_half, even_head[sub], out).astype(BF)

    _attention_pipeline([(sub, hd) for sub in range(q_ref.shape[0] // tq) for hd in range(MLA_HEADS)],
                        scores, values, emit)


def _mla_call(q, k, v, kc, vc, batch, tq):
    n, wq = q.shape
    wv = v.shape[1]
    wo = MLA_HEADS * MLA_V
    s = n // batch
    c = kc.shape[0] // batch
    q3, k3 = q.reshape(batch, s, wq), k.reshape(batch, s, wq)
    v3 = v.reshape(batch, s, wv)
    kc3, vc3 = kc.reshape(batch, c, wq), vc.reshape(batch, c, wv)
    step_q = MLA_TILES_PER_STEP * tq
    out = pl.pallas_call(
        functools.partial(_mla_kernel, tq=tq),
        grid=(batch, s // step_q),
        in_specs=[
            pl.BlockSpec((None, step_q, wq), lambda b, i: (b, i, 0)),
            pl.BlockSpec((None, s, wq), lambda b, i: (b, 0, 0)),
            pl.BlockSpec((None, s, wv), lambda b, i: (b, 0, 0)),
            pl.BlockSpec((None, c, wq), lambda b, i: (b, 0, 0)),
            pl.BlockSpec((None, c, wv), lambda b, i: (b, 0, 0)),
        ],
        out_specs=pl.BlockSpec((None, step_q, wo), lambda b, i: (b, i, 0)),
        out_shape=jax.ShapeDtypeStruct((batch, s, wo), BF),
        compiler_params=_params("parallel", "arbitrary"),
        name="latent_attn",
    )(q3, k3, v3, kc3, vc3)
    return out.reshape(n, wo)


def _ctx_attn_kernel(q_ref, k_ref, v_ref, mq_ref, mk_ref, mv_ref, ona_ref, omla_ref):
    c = q_ref.shape[0]
    first_half = lax.broadcasted_iota(jnp.int32, (c, LANE), 1) < NA_HEAD_DIM

    def attend(s, v):
        m = jnp.max(s, axis=-1, keepdims=True)
        p = jnp.exp2(s - m)
        return _dot(p.astype(BF), v) / jnp.sum(p, axis=-1, keepdims=True)

    def value_pair(ref, p):
        return jnp.where(first_half, ref[:, (2 * p) * LANE:(2 * p + 1) * LANE],
                         ref[:, (2 * p + 1) * LANE:(2 * p + 2) * LANE])

    for p in range(NA_WIDTH // LANE):
        lo = p * LANE
        q2 = q_ref[:, lo:lo + LANE]
        k2 = k_ref[:, lo:lo + LANE]
        v2 = value_pair(v_ref, p)
        outs = []
        for hh in range(2):
            qm = jnp.where(first_half if hh == 0 else ~first_half, q2, jnp.zeros_like(q2))
            outs.append(attend(_dot_nt(qm, k2), v2))
        ona_ref[:, lo:lo + LANE] = jnp.where(first_half, outs[0], outs[1]).astype(BF)
    for p in range(MLA_HEADS // 2):
        lo = p * LANE
        v2 = value_pair(mv_ref, p)
        outs = []
        for hh in range(2):
            hl = (2 * p + hh) * LANE
            outs.append(attend(_dot_nt(mq_ref[:, hl:hl + LANE], mk_ref[:, hl:hl + LANE]), v2))
        omla_ref[:, lo:lo + LANE] = jnp.where(first_half, outs[0], outs[1]).astype(BF)


def _ctx_attn_call(q, k, v, mq, mk, mv, batch):
    nc = q.shape[0]
    c = nc // batch
    ins = [q, k, v, mq, mk, mv]
    ins3 = [a.reshape(batch, c, a.shape[1]) for a in ins]
    spec = lambda w: pl.BlockSpec((None, c, w), lambda b: (b, 0, 0))
    ona, omla = pl.pallas_call(
        _ctx_attn_kernel,
        grid=(batch,),
        in_specs=[spec(a.shape[1]) for a in ins],
        out_specs=[spec(NA_WIDTH), spec(MLA_HEADS * MLA_V)],
        out_shape=[jax.ShapeDtypeStruct((batch, c, NA_WIDTH), BF),
                   jax.ShapeDtypeStruct((batch, c, MLA_HEADS * MLA_V), BF)],
        compiler_params=_params("parallel"),
        name="ctx_attn",
    )(*ins3)
    return ona.reshape(nc, NA_WIDTH), omla.reshape(nc, MLA_HEADS * MLA_V)


LRU_PAD = SUBLANE
LRU_CHUNK = 512
LRU_SEGS = SUBLANE
LRU_SEG_GAP = 4
LRU_TILES = LRU_WIDTH // LANE
LRU_SCAN_UNROLL = 8


def _lru_pitch(n):
    return n // LRU_SEGS + LRU_SEG_GAP


def _lru_pieces(t0, cn, n):
    seg = n // LRU_SEGS
    pitch = _lru_pitch(n)
    out = []
    for k in range(t0 // seg, (t0 + cn - 1) // seg + 1):
        lo, hi = max(t0, k * seg), min(t0 + cn, (k + 1) * seg)
        out.append((lo - t0, hi - t0, k * pitch + lo - k * seg))
    return out


def _lru_coeffs(x_ref, n, xpad_ref, a_ref, u_ref, cw_ref, cb_ref, wg_ref, bg_ref, sp2):
    w = LRU_WIDTH
    zeros = jnp.zeros((LRU_PAD, w), F32)
    xpad_ref[0:LRU_PAD, :] = zeros
    xpad_ref[LRU_PAD:LRU_PAD + n, :] = x_ref[...]
    xpad_ref[LRU_PAD + n:2 * LRU_PAD + n, :] = zeros
    left = (LRU_CONV_W - 1) // 2
    for c0 in range(0, n, LRU_CHUNK):
        cn = min(LRU_CHUNK, n - c0)
        xc = jnp.zeros((cn, w), F32) + cb_ref[...]
        for j in range(LRU_CONV_W):
            off = LRU_PAD + c0 + j - left
            xc = xc + cw_ref[j:j + 1, :] * xpad_ref[off:off + cn, :]
        t = jnp.tanh(_dot(xc.astype(BF), wg_ref[...]) + bg_ref[...])
        xh = 0.5 * xc
        for dr in range(2):
            a = jnp.exp2(sp2[:, dr * w:(dr + 1) * w] * (1.0 + t[:, dr * w:(dr + 1) * w]))
            y = 1.0 - a * a
            root = jnp.where(y > 0.0, y * lax.rsqrt(y), 0.0)
            u = root * ((1.0 + t[:, (2 + dr) * w:(3 + dr) * w]) * xh)
            for lo, hi, row in _lru_pieces(c0, cn, n):
                for j in range(LRU_TILES):
                    a_ref[dr, j, row:row + hi - lo, :] = a[lo:hi, j * LANE:(j + 1) * LANE]
                    u_ref[dr, j, row:row + hi - lo, :] = u[lo:hi, j * LANE:(j + 1) * LANE]


def _lru_scan(n, a_ref, u_ref, hs_ref, h0):
    seg = n // LRU_SEGS
    pitch = _lru_pitch(n)
    chains = [(d, j) for d in range(2) for j in range(LRU_TILES)]

    def rows(d, t):
        return pl.ds(t if d == 0 else seg - 1 - t, LRU_SEGS, stride=pitch)

    def sweep_transfer(t, carry):
        out = []
        for (d, j), (f, p) in zip(chains, carry):
            a = a_ref[d, j, rows(d, t), :]
            out.append((a * f + u_ref[d, j, rows(d, t), :], a * p))
        return tuple(out)

    zeros = jnp.zeros((LRU_SEGS, LANE), F32)
    ones = jnp.ones((LRU_SEGS, LANE), F32)
    transfer = lax.fori_loop(0, seg, sweep_transfer, tuple((zeros, ones) for _ in chains), unroll=LRU_SCAN_UNROLL)

    entries, exits = [], []
    for (d, j), (f, p) in zip(chains, transfer):
        state = h0[d][j]
        entry = [None] * LRU_SEGS
        for k in (range(LRU_SEGS) if d == 0 else range(LRU_SEGS - 1, -1, -1)):
            entry[k] = state
            state = p[k:k + 1, :] * state + f[k:k + 1, :]
        entries.append(jnp.concatenate(entry, axis=0))
        exits.append(state)

    def sweep_states(t, carry):
        out = []
        for (d, j), h in zip(chains, carry):
            h = a_ref[d, j, rows(d, t), :] * h + u_ref[d, j, rows(d, t), :]
            hs_ref[d, j, rows(d, t), :] = h
            out.append(h)
        return tuple(out)

    lax.fori_loop(0, seg, sweep_states, tuple(entries), unroll=LRU_SCAN_UNROLL)
    return [[exits[d * LRU_TILES + j] for j in range(LRU_TILES)] for d in range(2)]


def _lru_output(n, hs_ref, lg_ref, o_ref):
    for c0 in range(0, n, LRU_CHUNK):
        cn = min(LRU_CHUNK, n - c0)
        parts = []
        for lo, hi, row in _lru_pieces(c0, cn, n):
            parts.append(jnp.concatenate(
                [hs_ref[0, j, row:row + hi - lo, :] + hs_ref[1, j, row:row + hi - lo, :] for j in range(LRU_TILES)],
                axis=-1))
        hsum = parts[0] if len(parts) == 1 else jnp.concatenate(parts, axis=0)
        o_ref[c0:c0 + cn, :] = (_gelu_tanh(lg_ref[c0:c0 + cn, :]) * hsum).astype(BF)


def _lru_kernel(lx_ref, lg_ref, clx_ref, clg_ref, cw_ref, cb_ref, wg_ref, bg_ref, lam_ref,
                o_ref, oc_ref, xpad_ref, a_ref, u_ref, hs_ref, ac_ref, uc_ref):
    n = lx_ref.shape[0]
    nc = clx_ref.shape[0]
    nlam = -lam_ref[...]
    sp = jnp.maximum(nlam, 0.0) + jnp.log(1.0 + jnp.exp(-jnp.abs(nlam)))
    sp2 = (-0.5 * LRU_C * LOG2E) * sp
    zero = jnp.zeros((1, LANE), F32)
    _lru_coeffs(clx_ref, nc, xpad_ref, ac_ref, uc_ref, cw_ref, cb_ref, wg_ref, bg_ref, sp2)
    h_ctx = _lru_scan(nc, ac_ref, uc_ref, hs_ref, [[zero] * LRU_TILES] * 2)
    _lru_output(nc, hs_ref, clg_ref, oc_ref)
    _lru_coeffs(lx_ref, n, xpad_ref, a_ref, u_ref, cw_ref, cb_ref, wg_ref, bg_ref, sp2)
    _lru_scan(n, a_ref, u_ref, hs_ref, h_ctx)
    _lru_output(n, hs_ref, lg_ref, o_ref)


def _lru_call(lx, lg, clx, clg, l, wts, batch):
    n, w = lx.shape
    s = n // batch
    c = clx.shape[0] // batch
    seq = lambda a, ln: a.reshape(batch, ln, w)
    lat = pl.BlockSpec((None, s, w), lambda b: (b, 0, 0))
    ctx = pl.BlockSpec((None, c, w), lambda b: (b, 0, 0))
    stacked = [wts["conv_w"], wts["conv_b"], wts["w_gate"], wts["b_gate"], wts["lam"]]
    o, oc = pl.pallas_call(
        _lru_kernel,
        grid=(batch,),
        in_specs=[lat, lat, ctx, ctx] + [_layer_spec(a, l) for a in stacked],
        out_specs=[lat, ctx],
        out_shape=[jax.ShapeDtypeStruct((batch, s, w), BF), jax.ShapeDtypeStruct((batch, c, w), BF)],
        scratch_shapes=[pltpu.VMEM((s + 2 * LRU_PAD, w), F32)]
                       + [pltpu.VMEM((2, LRU_TILES, LRU_SEGS * _lru_pitch(ln), LANE), F32) for ln in (s, s, s, c, c)],
        compiler_params=_params("parallel"),
        name="rglru",
    )(seq(lx, s), seq(lg, s), seq(clx, c), seq(clg, c), *stacked)
    return o.reshape(n, w), oc.reshape(batch * c, w)


def _out_kernel(h_ref, na_ref, mla_ref, lru_ref, gm_ref, shf_ref, scf_ref, gf_ref, g_ref,
                wo_ref, wg_ref, wu_ref, wd_ref, o_ref, *, hid_chunk, sub_rows):
    hidden = wg_ref.shape[1]

    def residual_and_norm(r0):
        rs = slice(r0, r0 + sub_rows)
        o_cat = jnp.concatenate([na_ref[rs, :], mla_ref[rs, :], lru_ref[rs, :]], axis=-1)
        h1 = h_ref[rs, :] + gm_ref[...] * _dot(o_cat, wo_ref[...])
        y = h1 * lax.rsqrt(jnp.mean(h1 * h1, axis=-1, keepdims=True) + EPS) * g_ref[...]
        return h1, (y * (1.0 + scf_ref[...]) + shf_ref[...]).astype(BF)

    def gate_up(y, c0):
        return _dot(y, wg_ref[:, c0:c0 + hid_chunk]), _dot(y, wu_ref[:, c0:c0 + hid_chunk])

    starts = list(range(0, h_ref.shape[0], sub_rows))
    h1, y = residual_and_norm(starts[0])
    nxt = gate_up(y, 0)
    for si, r0 in enumerate(starts):
        acc = None
        h1_next = y_next = None
        for c0 in range(0, hidden, hid_chunk):
            gate, up = nxt
            if c0 + hid_chunk < hidden:
                nxt = gate_up(y, c0 + hid_chunk)
            elif si + 1 < len(starts):
                h1_next, y_next = residual_and_norm(starts[si + 1])
                nxt = gate_up(y_next, 0)
            d = _dot((_silu(gate) * up).astype(BF), wd_ref[c0:c0 + hid_chunk, :])
            acc = d if acc is None else acc + d
        o_ref[r0:r0 + sub_rows, :] = h1 + gf_ref[...] * acc
        h1, y = h1_next, y_next


def _out_call(h, na_o, mla_o, lru_o, mods, row_map, l, wts, tm, sub_rows):
    n, d = h.shape
    row = lambda w: pl.BlockSpec((tm, w), lambda i: (i, 0))
    stacked = [wts["g_ffn"], wts["w_out"], wts["ffn_gate"], wts["ffn_up"], wts["ffn_down"]]
    hidden = wts["ffn_gate"].shape[-1]
    hid_chunk = 2 * LANE if hidden % (2 * LANE) == 0 else hidden
    return pl.pallas_call(
        functools.partial(_out_kernel, hid_chunk=hid_chunk, sub_rows=sub_rows),
        grid=(n // tm,),
        in_specs=[row(d), row(na_o.shape[1]), row(mla_o.shape[1]), row(lru_o.shape[1])]
                 + [_mod_spec(mods, l, which, row_map) for which in (2, 3, 4, 5)]
                 + [_layer_spec(a, l) for a in stacked],
        out_specs=row(d),
        out_shape=jax.ShapeDtypeStruct((n, d), F32),
        compiler_params=_params("parallel"),
        name="out_ffn",
    )(h, na_o, mla_o, lru_o, mods, mods, mods, mods, *stacked)


def _rope_perm():
    j = np.arange(MLA_ROPE)
    first = (j % (MLA_ROPE // 2)) < MLA_ROPE // 4
    partner = np.where(first, j + MLA_ROPE // 4, j - MLA_ROPE // 4)
    sign = np.where(first, -1.0, 1.0).astype(np.float32)
    return partner, sign


def _rope_tables(s):
    t = np.arange(s)
    row = (t // GRID_W).astype(np.float32)
    col = (t % GRID_W).astype(np.float32)
    n_freq = MLA_ROPE // 4
    inv = jnp.asarray(ROPE_BASE, F32) ** (-jnp.arange(n_freq, dtype=F32) / n_freq)
    ar = jnp.asarray(row)[:, None] * inv
    ac = jnp.asarray(col)[:, None] * inv
    ang = jnp.concatenate([ar, ar, ac, ac], axis=-1)
    pad = LANE - MLA_NOPE - MLA_ROPE
    cos = jnp.concatenate([jnp.ones((s, MLA_NOPE), F32), jnp.cos(ang), jnp.zeros((s, pad), F32)], axis=-1)
    sin = jnp.concatenate([jnp.zeros((s, MLA_NOPE), F32), jnp.sin(ang), jnp.zeros((s, pad), F32)], axis=-1)
    return cos, sin


def _lane_tile(nope, rope):
    lead = nope.shape[:-1]
    pad = jnp.zeros(lead + (LANE - MLA_NOPE - MLA_ROPE,), nope.dtype)
    return jnp.concatenate([nope, rope, pad], axis=-1)


def _prep_weights(p):
    partner, sign = _rope_perm()
    w_in = p["w_in"]
    depth, d, _ = w_in.shape
    cols = np.cumsum((NA_WIDTH, NA_WIDTH, NA_WIDTH, MLA_Q_RANK, MLA_KV_RANK, MLA_ROPE, LRU_WIDTH, LRU_WIDTH))
    w_kr = w_in[..., cols[4]:cols[5]]
    zn = jnp.zeros((depth, d, MLA_NOPE), F32)
    w_all = jnp.concatenate([
        w_in[..., :cols[4]], _lane_tile(zn, w_kr), _lane_tile(zn, w_kr[..., partner]), w_in[..., cols[5]:]], axis=-1)
    assert w_all.shape[-1] == W_ALL

    wq = p["mla_w_qb"].reshape(depth, MLA_Q_RANK, MLA_HEADS, MLA_NOPE + MLA_ROPE)
    wq_n, wq_r = wq[..., :MLA_NOPE], wq[..., MLA_NOPE:]
    w_qb = jnp.concatenate([
        _lane_tile(wq_n, wq_r).reshape(depth, MLA_Q_RANK, MLA_QK_W),
        _lane_tile(jnp.zeros_like(wq_n), wq_r[..., partner]).reshape(depth, MLA_Q_RANK, MLA_QK_W)], axis=-1)

    wkv = p["mla_w_kvb"].reshape(depth, MLA_KV_RANK, MLA_HEADS, MLA_NOPE + MLA_V)
    wk_n, wv = wkv[..., :MLA_NOPE], wkv[..., MLA_NOPE:]
    w_kvb = jnp.concatenate([
        _lane_tile(wk_n, jnp.zeros(wk_n.shape[:-1] + (MLA_ROPE,), F32)).reshape(depth, MLA_KV_RANK, MLA_QK_W),
        wv.reshape(depth, MLA_KV_RANK, MLA_HEADS * MLA_V)], axis=-1)

    qg, kg = p["mla_q_g"], p["mla_k_g"]
    z_n, z_r = jnp.zeros((depth, MLA_NOPE), F32), jnp.zeros((depth, MLA_ROPE), F32)
    sgn = jnp.asarray(sign)
    zrow = jnp.zeros((depth, LANE), F32)
    g_mla = jnp.stack([
        _lane_tile(qg[:, :MLA_NOPE], qg[:, MLA_NOPE:]) * MLA_SCALE,
        _lane_tile(z_n, sgn * qg[:, MLA_NOPE:][:, partner]) * MLA_SCALE,
        _lane_tile(kg[:, :MLA_NOPE], z_r),
        _lane_tile(z_n, kg[:, MLA_NOPE:]),
        _lane_tile(z_n, sgn * kg[:, MLA_NOPE:][:, partner]),
        zrow, zrow, zrow], axis=1)

    g_na = jnp.concatenate([jnp.tile(p["na_q_g"], (1, NA_HEADS)) * NA_SCALE, jnp.tile(p["na_k_g"], (1, NA_HEADS))],
                           axis=-1)[:, None]

    wa, wx = p["lru_w_a"], p["lru_w_x"]
    wg4 = jnp.stack([wa[:, 0], wa[:, 1], wx[:, 0], wx[:, 1]], axis=1)
    eye = jnp.eye(LRU_HEADS, dtype=F32)
    w_gate = (wg4[:, :, :, :, None, :] * eye[None, None, :, None, :, None])
    w_gate = w_gate.transpose(0, 2, 3, 1, 4, 5).reshape(depth, LRU_WIDTH, 4 * LRU_WIDTH)
    ba, bx = p["lru_b_a"], p["lru_b_x"]
    b_gate = jnp.concatenate([ba[:, 0], ba[:, 1], bx[:, 0], bx[:, 1]], axis=-1)[:, None]

    return {
        "g_mix": p["norm_mix_g"][:, None], "g_ffn": p["norm_ffn_g"][:, None],
        "w_all": w_all.astype(BF), "w_qb": w_qb.astype(BF), "w_kvb": w_kvb.astype(BF),
        "g_na": g_na, "g_cq": p["mla_cq_g"][:, None], "g_ckv": p["mla_ckv_g"][:, None], "g_mla": g_mla,
        "conv_w": p["lru_conv_w"], "conv_b": p["lru_conv_b"][:, None],
        "w_gate": (0.5 * w_gate).astype(BF), "b_gate": 0.5 * b_gate, "lam": p["lru_lambda"].reshape(depth, 1, 2 * LRU_WIDTH),
        "w_out": p["w_out"].astype(BF), "ffn_gate": p["ffn_w_gate"].astype(BF),
        "ffn_up": p["ffn_w_up"].astype(BF), "ffn_down": p["ffn_w_down"].astype(BF),
    }


def kernel(x, c, ctx, c_ctx, ada_w, ada_b, norm_mix_g, norm_ffn_g, w_in, na_q_g, na_k_g, na_rpb, mla_cq_g, mla_w_qb, mla_ckv_g, mla_w_kvb, mla_q_g, mla_k_g, lru_conv_w, lru_conv_b, lru_w_a, lru_b_a, lru_w_x, lru_b_x, lru_lambda, w_out, ffn_w_gate, ffn_w_up, ffn_w_down):
    batch, s, d = x.shape
    cl = ctx.shape[1]
    depth = ada_w.shape[0]
    rows = s // GRID_W
    assert s % GRID_W == 0 and rows % (NA_BLOCKS_PER_STEP * NA_QROWS) == 0 and rows >= 2 * NA_KROWS
    p = dict(norm_mix_g=norm_mix_g, norm_ffn_g=norm_ffn_g, w_in=w_in, na_q_g=na_q_g, na_k_g=na_k_g,
             mla_cq_g=mla_cq_g, mla_w_qb=mla_w_qb, mla_ckv_g=mla_ckv_g, mla_w_kvb=mla_w_kvb, mla_q_g=mla_q_g,
             mla_k_g=mla_k_g, lru_conv_w=lru_conv_w, lru_conv_b=lru_conv_b, lru_w_a=lru_w_a, lru_b_a=lru_b_a,
             lru_w_x=lru_w_x, lru_b_x=lru_b_x, lru_lambda=lru_lambda, w_out=w_out, ffn_w_gate=ffn_w_gate,
             ffn_w_up=ffn_w_up, ffn_w_down=ffn_w_down)

    sub = 512 if s % 512 == 0 else GRID_W * NA_QROWS
    tm = 2 * sub if s % (2 * sub) == 0 else sub
    subc = sub if (batch * cl) % sub == 0 else cl
    tmc = 2 * subc if (batch * cl) % (2 * subc) == 0 else subc
    tq = 512

    nmod = -(-(batch + 1) // SUBLANE) * SUBLANE
    cvec = jnp.concatenate([c, c_ctx[None], jnp.zeros((nmod - batch - 1, d), F32)], axis=0)
    mods = _ada_call(cvec, ada_w, ada_b).reshape(depth, nmod, 6, 1, d)
    tiles_per_batch = s // tm
    lat_row = lambda i: i // tiles_per_batch
    ctx_row = lambda i: batch

    wts = _prep_weights(p)
    na_bias = _na_bias_tables(na_rpb, rows)
    cos, sin = _rope_tables(s)
    cos_c = jnp.concatenate([jnp.ones((tmc, MLA_NOPE + MLA_ROPE), F32),
                             jnp.zeros((tmc, LANE - MLA_NOPE - MLA_ROPE), F32)], axis=-1)
    sin_c = jnp.zeros((tmc, LANE), F32)

    h = x.reshape(batch * s, d)
    hc = ctx.reshape(batch * cl, d)
    for l in range(depth):
        last = l == depth - 1
        q, k, v, mq, mk, mv, lx, lg = _in_call(h, mods, lat_row, l, wts, cos, sin, tm, sub // 2)
        cq_, ck, cv, cmq, cmk, cmv, clx, clg = _in_call(hc, mods, ctx_row, l, wts, cos_c, sin_c, tmc, subc // 2)

        na_o = _na_call(q, k, v, ck, cv, na_bias, l, batch)
        mla_o = _mla_call(mq, mk, mv, cmk, cmv, batch, tq)
        lru_o, lru_c = _lru_call(lx, lg, clx, clg, l, wts, batch)
        h = _out_call(h, na_o, mla_o, lru_o, mods, lat_row, l, wts, tm, sub)

        if not last:
            na_c, mla_c = _ctx_attn_call(cq_, ck, cv, cmq, cmk, cmv, batch)
            hc = _out_call(hc, na_c, mla_c, lru_c, mods, ctx_row, l, wts, tmc, subc)
    return h.reshape(batch, s, d)
```

```python
import functools
import math

import numpy as np
import jax
import jax.numpy as jnp
from jax import lax
from jax.experimental import pallas as pl
from jax.experimental.pallas import tpu as pltpu

GRID_W = 64
NA_HEADS = 6
NA_HEAD_DIM = 64
NA_WIDTH = NA_HEADS * NA_HEAD_DIM
NA_WIN_H = 8
NA_WIN_W = 16
MLA_HEADS = 6
MLA_Q_RANK = 256
MLA_KV_RANK = 128
MLA_NOPE = 64
MLA_ROPE = 32
MLA_V = 64
LRU_WIDTH = 256
LRU_HEADS = 4
LRU_BLOCK = LRU_WIDTH // LRU_HEADS
LRU_CONV_W = 4
LRU_C = 8.0
ROPE_BASE = 10000.0
EPS = 1e-6
LOG2E = math.log2(math.e)
NA_SCALE = NA_HEAD_DIM ** -0.5 * LOG2E
MLA_SCALE = (MLA_NOPE + MLA_ROPE) ** -0.5 * LOG2E

LANE = 128
SUBLANE = 8
MLA_QK_W = MLA_HEADS * LANE
NEG_BIG = -1e30
VMEM_LIMIT = 56 * 1024 * 1024

NA_QROWS = 4
NA_KROWS = NA_QROWS + NA_WIN_H
NA_CLASSES = 3
NA_BLOCKS_PER_STEP = 4

C_Q, C_K, C_V = 0, NA_WIDTH, 2 * NA_WIDTH
C_CQ = 3 * NA_WIDTH
C_CKV = C_CQ + MLA_Q_RANK
C_KR = C_CKV + MLA_KV_RANK
C_KRP = C_KR + LANE
C_LX = C_KRP + LANE
C_LG = C_LX + LRU_WIDTH
W_ALL = C_LG + LRU_WIDTH

BF = jnp.bfloat16
F32 = jnp.float32


def _dot(a, b):
    return jnp.dot(a, b, preferred_element_type=F32)


def _dot_nt(a, b):
    return lax.dot_general(a, b, (((1,), (1,)), ((), ())), preferred_element_type=F32)


def _silu(x):
    hx = 0.5 * x
    return hx + hx * jnp.tanh(hx)


def _gelu_tanh(x):
    return 0.5 * x * (1.0 + jnp.tanh(0.7978845608028654 * (x + 0.044715 * (x * x * x))))


def _params(*sem):
    return pltpu.CompilerParams(dimension_semantics=sem, vmem_limit_bytes=VMEM_LIMIT)


def _const_spec(shape):
    nd = len(shape)
    return pl.BlockSpec(shape, lambda *_: (0,) * nd, pipeline_mode=pl.Buffered(1))


def _layer_spec(arr, l):
    nd = arr.ndim
    return pl.BlockSpec((None,) + arr.shape[1:], lambda *_: (l,) + (0,) * (nd - 1), pipeline_mode=pl.Buffered(1))


def _mod_spec(mods, l, which, row_map):
    d = mods.shape[-1]
    return pl.BlockSpec((None, None, None, 1, d), lambda i: (l, row_map(i), which, 0, 0))


def _ada_kernel(c_ref, w_ref, b_ref, o_ref):
    s = _silu(c_ref[...]).astype(BF)
    o_ref[...] = _dot(s, w_ref[...].astype(BF)) + b_ref[...]


def _ada_call(cvec, ada_w, ada_b):
    depth, d, d6 = ada_w.shape
    r = cvec.shape[0]
    tn = 1024
    return pl.pallas_call(
        _ada_kernel,
        grid=(depth, d6 // tn),
        in_specs=[
            pl.BlockSpec((r, d), lambda l, j: (0, 0)),
            pl.BlockSpec((None, d, tn), lambda l, j: (l, 0, j)),
            pl.BlockSpec((None, 1, tn), lambda l, j: (l, 0, j)),
        ],
        out_specs=pl.BlockSpec((None, r, tn), lambda l, j: (l, 0, j)),
        out_shape=jax.ShapeDtypeStruct((depth, r, d6), F32),
        compiler_params=_params("parallel", "parallel"),
        name="adaln",
    )(cvec, ada_w, ada_b.reshape(depth, 1, d6))


def _in_kernel(h_ref, sh_ref, sc_ref, g_ref, wall_ref, wqb_ref, wkvb_ref, gna_ref, gcq_ref, gckv_ref,
               gm_ref, cos_ref, sin_ref,
               q_ref, k_ref, v_ref, mq_ref, mk_ref, mv_ref, lx_ref, lg_ref, *, sub_rows):
    lane = lax.broadcasted_iota(jnp.int32, (sub_rows, LANE), 1)
    na_first = lane < NA_HEAD_DIM
    nope = lane < MLA_NOPE

    def seg_rsqrt(t, in_first, n_first, n_second):
        t2 = t * t
        s1 = jnp.sum(jnp.where(in_first, t2, 0.0), axis=-1, keepdims=True) * (1.0 / n_first)
        s2 = jnp.sum(jnp.where(in_first, 0.0, t2), axis=-1, keepdims=True) * (1.0 / n_second)
        return lax.rsqrt(jnp.where(in_first, s1, s2) + EPS)

    def store_values(ref, rs, vals):
        for p in range(vals.shape[1] // LANE):
            vp = vals[:, p * LANE:(p + 1) * LANE]
            ref[rs, (2 * p) * LANE:(2 * p + 1) * LANE] = jnp.where(na_first, vp, 1.0).astype(BF)
            ref[rs, (2 * p + 1) * LANE:(2 * p + 2) * LANE] = jnp.where(na_first, 1.0, vp).astype(BF)

    def project(r0):
        x = h_ref[r0:r0 + sub_rows, :]
        y = x * lax.rsqrt(jnp.mean(x * x, axis=-1, keepdims=True) + EPS) * g_ref[...]
        y = y * (1.0 + sc_ref[...]) + sh_ref[...]
        return _dot(y.astype(BF), wall_ref[...])

    def head_groups(r0, u):
        rs = slice(r0, r0 + sub_rows)
        for p in range(2 * NA_WIDTH // LANE):
            lo = p * LANE
            t = u[:, C_Q + lo:C_Q + lo + LANE]
            tn = (t * seg_rsqrt(t, na_first, NA_HEAD_DIM, NA_HEAD_DIM) * gna_ref[:, lo:lo + LANE]).astype(BF)
            if lo < NA_WIDTH:
                q_ref[rs, lo:lo + LANE] = tn
            else:
                k_ref[rs, lo - NA_WIDTH:lo - NA_WIDTH + LANE] = tn
        store_values(v_ref, rs, u[:, C_V:C_V + NA_WIDTH])

        cos = cos_ref[rs, :]
        sin = sin_ref[rs, :]
        cq = u[:, C_CQ:C_CQ + MLA_Q_RANK]
        cqn = cq * lax.rsqrt(jnp.mean(cq * cq, axis=-1, keepdims=True) + EPS) * gcq_ref[...]
        qq = _dot(cqn.astype(BF), wqb_ref[...])
        a_q = gm_ref[0:1, :] * cos
        b_q = gm_ref[1:2, :] * sin
        for hd in range(MLA_HEADS):
            lo = hd * LANE
            qr = qq[:, lo:lo + LANE]
            qp = qq[:, MLA_QK_W + lo:MLA_QK_W + lo + LANE]
            mq_ref[rs, lo:lo + LANE] = (seg_rsqrt(qr, nope, MLA_NOPE, MLA_ROPE) * (qr * a_q + qp * b_q)).astype(BF)

        ckv = u[:, C_CKV:C_CKV + MLA_KV_RANK]
        ckvn = ckv * lax.rsqrt(jnp.mean(ckv * ckv, axis=-1, keepdims=True) + EPS) * gckv_ref[...]
        kv = _dot(ckvn.astype(BF), wkvb_ref[...])
        krr = u[:, C_KR:C_KR + LANE]
        krp = u[:, C_KRP:C_KRP + LANE]
        msr = jnp.sum(krr * krr, axis=-1, keepdims=True) * (1.0 / MLA_ROPE)
        kr = lax.rsqrt(msr + EPS) * (krr * (gm_ref[3:4, :] * cos) + krp * (gm_ref[4:5, :] * sin))
        for hd in range(MLA_HEADS):
            lo = hd * LANE
            kn = kv[:, lo:lo + LANE]
            ms = jnp.sum(kn * kn, axis=-1, keepdims=True) * (1.0 / MLA_NOPE)
            mk_ref[rs, lo:lo + LANE] = (kn * lax.rsqrt(ms + EPS) * gm_ref[2:3, :] + kr).astype(BF)
        store_values(mv_ref, rs, kv[:, MLA_QK_W:])

        lx_ref[rs, :] = u[:, C_LX:C_LX + LRU_WIDTH]
        lg_ref[rs, :] = u[:, C_LG:C_LG + LRU_WIDTH]

    starts = list(range(0, h_ref.shape[0], sub_rows))
    u_next = project(starts[0])
    for si, r0 in enumerate(starts):
        u = u_next
        if si + 1 < len(starts):
            u_next = project(starts[si + 1])
        head_groups(r0, u)


def _in_call(h, mods, row_map, l, wts, cos, sin, tm, sub_rows):
    n, d = h.shape
    tab_tiles = cos.shape[0] // tm
    row = lambda w: pl.BlockSpec((tm, w), lambda i: (i, 0))
    tab = pl.BlockSpec((tm, LANE), lambda i: (i % tab_tiles, 0))
    stacked = [wts["g_mix"], wts["w_all"], wts["w_qb"], wts["w_kvb"], wts["g_na"], wts["g_cq"], wts["g_ckv"],
               wts["g_mla"]]
    out_w = [(NA_WIDTH, BF), (NA_WIDTH, BF), (NA_HEADS * LANE, BF), (MLA_QK_W, BF), (MLA_QK_W, BF),
             (MLA_HEADS * LANE, BF), (LRU_WIDTH, F32), (LRU_WIDTH, F32)]
    return pl.pallas_call(
        functools.partial(_in_kernel, sub_rows=sub_rows),
        grid=(n // tm,),
        in_specs=[row(d), _mod_spec(mods, l, 0, row_map), _mod_spec(mods, l, 1, row_map)]
                 + [_layer_spec(a, l) for a in stacked] + [tab, tab],
        out_specs=[row(w) for w, _ in out_w],
        out_shape=[jax.ShapeDtypeStruct((n, w), dt) for w, dt in out_w],
        compiler_params=_params("parallel"),
        name="in_proj",
    )(h, mods, mods, *stacked, cos, sin)


def _attention_pipeline(work, scores, values, emit):
    s_next = scores(work[0])
    for i, item in enumerate(work):
        s_cur = s_next
        if i + 1 < len(work):
            s_next = scores(work[i + 1])
        m = functools.reduce(jnp.maximum, [jnp.max(s, axis=-1, keepdims=True) for s in s_cur])
        o = sum(_dot(jnp.exp2(s - m).astype(BF), v) for s, v in zip(s_cur, values(item)))
        emit(item, o / pltpu.roll(o, LANE // 2, 1))


def _na_block_geometry(rows):
    nblk = rows // NA_QROWS
    half = NA_WIN_H // 2
    ks = np.clip(np.arange(nblk) * NA_QROWS - half, 0, rows - NA_KROWS)
    pats = []
    for j in range(nblk):
        r = j * NA_QROWS + np.arange(NA_QROWS)
        rs = np.clip(r - half, 0, rows - NA_WIN_H)
        assert ks[j] <= rs.min() and rs.max() + NA_WIN_H <= ks[j] + NA_KROWS
        pats.append(tuple(zip((r - ks[j]).tolist(), (rs - ks[j]).tolist())))
    cls = [0 if j == 0 else (2 if j == nblk - 1 else 1) for j in range(nblk)]
    by_cls = {}
    for j in range(nblk):
        assert by_cls.setdefault(cls[j], pats[j]) == pats[j]
    return [by_cls[c] for c in range(NA_CLASSES)]


def _na_kernel(q_ref, k_ref, v_ref, kc_ref, vc_ref, bias_ref, o_ref, *, rows):
    nblk = rows // NA_QROWS
    nq = NA_QROWS * GRID_W
    nk = NA_KROWS * GRID_W
    first_half = lax.broadcasted_iota(jnp.int32, (nq, LANE), 1) < NA_HEAD_DIM

    def block_geometry(sub):
        j = pl.program_id(1) * NA_BLOCKS_PER_STEP + sub
        ks = jnp.clip(j * NA_QROWS - NA_WIN_H // 2, 0, rows - NA_KROWS)
        cls = jnp.where(j == 0, 0, jnp.where(j == nblk - 1, 2, 1))
        return pl.multiple_of(ks * GRID_W, GRID_W), cls

    geom = [block_geometry(sub) for sub in range(NA_BLOCKS_PER_STEP)]

    def scores(item):
        sub, hd = item
        start, cls = geom[sub]
        lo = (hd // 2) * LANE
        q2 = q_ref[sub * nq:(sub + 1) * nq, lo:lo + LANE]
        qm = jnp.where(first_half if hd % 2 == 0 else ~first_half, q2, jnp.zeros_like(q2))
        s_w = _dot_nt(qm, k_ref[pl.ds(start, nk), lo:lo + LANE]) + bias_ref[cls, hd]
        return s_w, _dot_nt(qm, kc_ref[:, lo:lo + LANE])

    def values(item):
        sub, hd = item
        return v_ref[pl.ds(geom[sub][0], nk), hd * LANE:(hd + 1) * LANE], vc_ref[:, hd * LANE:(hd + 1) * LANE]

    even_head = {}

    def emit(item, out):
        sub, hd = item
        if hd % 2 == 0:
            even_head[sub] = out
        else:
            lo = (hd // 2) * LANE
            o_ref[sub * nq:(sub + 1) * nq, lo:lo + LANE] = jnp.where(first_half, even_head[sub], out).astype(BF)

    _attention_pipeline([(sub, hd) for sub in range(NA_BLOCKS_PER_STEP) for hd in range(NA_HEADS)],
                        scores, values, emit)


def _na_call(q, k, v, kc, vc, bias, l, batch):
    n, w = q.shape
    s = n // batch
    c = kc.shape[0] // batch
    rows = s // GRID_W
    wv = v.shape[1]
    q3, k3, v3 = q.reshape(batch, s, w), k.reshape(batch, s, w), v.reshape(batch, s, wv)
    kc3, vc3 = kc.reshape(batch, c, w), vc.reshape(batch, c, wv)
    nq = NA_BLOCKS_PER_STEP * NA_QROWS * GRID_W
    out = pl.pallas_call(
        functools.partial(_na_kernel, rows=rows),
        grid=(batch, rows // (NA_BLOCKS_PER_STEP * NA_QROWS)),
        in_specs=[
            pl.BlockSpec((None, nq, w), lambda b, j: (b, j, 0)),
            pl.BlockSpec((None, s, w), lambda b, j: (b, 0, 0)),
            pl.BlockSpec((None, s, wv), lambda b, j: (b, 0, 0)),
            pl.BlockSpec((None, c, w), lambda b, j: (b, 0, 0)),
            pl.BlockSpec((None, c, wv), lambda b, j: (b, 0, 0)),
            _layer_spec(bias, l),
        ],
        out_specs=pl.BlockSpec((None, nq, w), lambda b, j: (b, j, 0)),
        out_shape=jax.ShapeDtypeStruct((batch, s, w), BF),
        compiler_params=_params("parallel", "arbitrary"),
        name="nbr_attn",
    )(q3, k3, v3, kc3, vc3, bias)
    return out.reshape(n, w)


def _na_bias_tables(rpb, rows):
    depth = rpb.shape[0]
    qc = np.arange(GRID_W)
    kcol = np.arange(GRID_W)
    cs = np.clip(qc - NA_WIN_W // 2, 0, GRID_W - NA_WIN_W)
    ok = (kcol[None, :] >= cs[:, None]) & (kcol[None, :] < cs[:, None] + NA_WIN_W)
    col_rel = np.clip(kcol[None, :] - qc[:, None], 1 - NA_WIN_W, NA_WIN_W - 1) + NA_WIN_W - 1
    sel = (col_rel[None] == np.arange(2 * NA_WIN_W - 1)[:, None, None]).astype(np.float32)
    t = jnp.einsum("lhrd,dqk->lhrqk", rpb.astype(F32), jnp.asarray(sel), precision=lax.Precision.HIGHEST)
    t = jnp.where(ok, t * LOG2E, NEG_BIG)
    neg = jnp.full((depth, NA_HEADS, 1, GRID_W, GRID_W), NEG_BIG, F32)
    n_dr = 2 * NA_WIN_H - 1
    t_ext = jnp.concatenate([t, neg], axis=2)
    pair_ids = {}

    def pair_id(dr_a, dr_b):
        return pair_ids.setdefault((dr_a, dr_b), len(pair_ids))

    layout = []
    for pat in _na_block_geometry(rows):
        layout.append([])
        for qo, wo in pat:
            dr = [kr - qo + NA_WIN_H - 1 if wo <= kr < wo + NA_WIN_H else n_dr for kr in range(NA_KROWS)]
            layout[-1].append([pair_id(dr[2 * j], dr[2 * j + 1]) for j in range(NA_KROWS // 2)])
    pairs = [jnp.concatenate([t_ext[:, :, a], t_ext[:, :, b]], axis=-1) for a, b in pair_ids]
    classes = []
    for cls in layout:
        qrows = [jnp.concatenate([pairs[pid] for pid in tiles], axis=-1) for tiles in cls]
        classes.append(jnp.concatenate(qrows, axis=-2))
    return jnp.stack(classes, axis=1)


MLA_TILES_PER_STEP = 2


def _mla_kernel(q_ref, k_ref, v_ref, kc_ref, vc_ref, o_ref, *, tq):
    first_half = lax.broadcasted_iota(jnp.int32, (tq, LANE), 1) < MLA_V

    def scores(item):
        sub, hd = item
        hl = hd * LANE
        qh = q_ref[sub * tq:(sub + 1) * tq, hl:hl + LANE]
        return _dot_nt(qh, k_ref[:, hl:hl + LANE]), _dot_nt(qh, kc_ref[:, hl:hl + LANE])

    def values(item):
        hl = item[1] * LANE
        return v_ref[:, hl:hl + LANE], vc_ref[:, hl:hl + LANE]

    even_head = {}

    def emit(item, out):
        sub, hd = item
        if hd % 2 == 0:
            even_head[sub] = out
        else:
            lo = (hd // 2) * LANE
            o_ref[sub * tq:(sub + 1) * tq, lo:lo + LANE] = jnp.where(first_half, even_head[sub], out).astype(BF)

    _attention_pipeline([(sub, hd) for sub in range(q_ref.shape[0] // tq) for hd in range(MLA_HEADS)],
                        scores, values, emit)


def _mla_call(q, k, v, kc, vc, batch, tq):
    n, wq = q.shape
    wv = v.shape[1]
    wo = MLA_HEADS * MLA_V
    s = n // batch
    c = kc.shape[0] // batch
    q3, k3 = q.reshape(batch, s, wq), k.reshape(batch, s, wq)
    v3 = v.reshape(batch, s, wv)
    kc3, vc3 = kc.reshape(batch, c, wq), vc.reshape(batch, c, wv)
    step_q = MLA_TILES_PER_STEP * tq
    out = pl.pallas_call(
        functools.partial(_mla_kernel, tq=tq),
        grid=(batch, s // step_q),
        in_specs=[
            pl.BlockSpec((None, step_q, wq), lambda b, i: (b, i, 0)),
            pl.BlockSpec((None, s, wq), lambda b, i: (b, 0, 0)),
            pl.BlockSpec((None, s, wv), lambda b, i: (b, 0, 0)),
            pl.BlockSpec((None, c, wq), lambda b, i: (b, 0, 0)),
            pl.BlockSpec((None, c, wv), lambda b, i: (b, 0, 0)),
        ],
        out_specs=pl.BlockSpec((None, step_q, wo), lambda b, i: (b, i, 0)),
        out_shape=jax.ShapeDtypeStruct((batch, s, wo), BF),
        compiler_params=_params("parallel", "arbitrary"),
        name="latent_attn",
    )(q3, k3, v3, kc3, vc3)
    return out.reshape(n, wo)


def _ctx_attn_kernel(q_ref, k_ref, v_ref, mq_ref, mk_ref, mv_ref, ona_ref, omla_ref):
    c = q_ref.shape[0]
    first_half = lax.broadcasted_iota(jnp.int32, (c, LANE), 1) < NA_HEAD_DIM

    def attend(s, v):
        m = jnp.max(s, axis=-1, keepdims=True)
        p = jnp.exp2(s - m)
        return _dot(p.astype(BF), v) / jnp.sum(p, axis=-1, keepdims=True)

    def value_pair(ref, p):
        return jnp.where(first_half, ref[:, (2 * p) * LANE:(2 * p + 1) * LANE],
                         ref[:, (2 * p + 1) * LANE:(2 * p + 2) * LANE])

    for p in range(NA_WIDTH // LANE):
        lo = p * LANE
        q2 = q_ref[:, lo:lo + LANE]
        k2 = k_ref[:, lo:lo + LANE]
        v2 = value_pair(v_ref, p)
        outs = []
        for hh in range(2):
            qm = jnp.where(first_half if hh == 0 else ~first_half, q2, jnp.zeros_like(q2))
            outs.append(attend(_dot_nt(qm, k2), v2))
        ona_ref[:, lo:lo + LANE] = jnp.where(first_half, outs[0], outs[1]).astype(BF)
    for p in range(MLA_HEADS // 2):
        lo = p * LANE
        v2 = value_pair(mv_ref, p)
        outs = []
        for hh in range(2):
            hl = (2 * p + hh) * LANE
            outs.append(attend(_dot_nt(mq_ref[:, hl:hl + LANE], mk_ref[:, hl:hl + LANE]), v2))
        omla_ref[:, lo:lo + LANE] = jnp.where(first_half, outs[0], outs[1]).astype(BF)


def _ctx_attn_call(q, k, v, mq, mk, mv, batch):
    nc = q.shape[0]
    c = nc // batch
    ins = [q, k, v, mq, mk, mv]
    ins3 = [a.reshape(batch, c, a.shape[1]) for a in ins]
    spec = lambda w: pl.BlockSpec((None, c, w), lambda b: (b, 0, 0))
    ona, omla = pl.pallas_call(
        _ctx_attn_kernel,
        grid=(batch,),
        in_specs=[spec(a.shape[1]) for a in ins],
        out_specs=[spec(NA_WIDTH), spec(MLA_HEADS * MLA_V)],
        out_shape=[jax.ShapeDtypeStruct((batch, c, NA_WIDTH), BF),
                   jax.ShapeDtypeStruct((batch, c, MLA_HEADS * MLA_V), BF)],
        compiler_params=_params("parallel"),
        name="ctx_attn",
    )(*ins3)
    return ona.reshape(nc, NA_WIDTH), omla.reshape(nc, MLA_HEADS * MLA_V)


LRU_PAD = SUBLANE
LRU_CHUNK = 512
LRU_SEGS = SUBLANE
LRU_SEG_GAP = 4
LRU_TILES = LRU_WIDTH // LANE
LRU_SCAN_UNROLL = 8


def _lru_pitch(n):
    return n // LRU_SEGS + LRU_SEG_GAP


def _lru_pieces(t0, cn, n):
    seg = n // LRU_SEGS
    pitch = _lru_pitch(n)
    out = []
    for k in range(t0 // seg, (t0 + cn - 1) // seg + 1):
        lo, hi = max(t0, k * seg), min(t0 + cn, (k + 1) * seg)
        out.append((lo - t0, hi - t0, k * pitch + lo - k * seg))
    return out


def _lru_coeffs(x_ref, n, xpad_ref, a_ref, u_ref, cw_ref, cb_ref, wg_ref, bg_ref, sp2):
    w = LRU_WIDTH
    zeros = jnp.zeros((LRU_PAD, w), F32)
    xpad_ref[0:LRU_PAD, :] = zeros
    xpad_ref[LRU_PAD:LRU_PAD + n, :] = x_ref[...]
    xpad_ref[LRU_PAD + n:2 * LRU_PAD + n, :] = zeros
    left = (LRU_CONV_W - 1) // 2
    for c0 in range(0, n, LRU_CHUNK):
        cn = min(LRU_CHUNK, n - c0)
        xc = jnp.zeros((cn, w), F32) + cb_ref[...]
        for j in range(LRU_CONV_W):
            off = LRU_PAD + c0 + j - left
            xc = xc + cw_ref[j:j + 1, :] * xpad_ref[off:off + cn, :]
        t = jnp.tanh(_dot(xc.astype(BF), wg_ref[...]) + bg_ref[...])
        xh = 0.5 * xc
        for dr in range(2):
            a = jnp.exp2(sp2[:, dr * w:(dr + 1) * w] * (1.0 + t[:, dr * w:(dr + 1) * w]))
            y = 1.0 - a * a
            root = jnp.where(y > 0.0, y * lax.rsqrt(y), 0.0)
            u = root * ((1.0 + t[:, (2 + dr) * w:(3 + dr) * w]) * xh)
            for lo, hi, row in _lru_pieces(c0, cn, n):
                for j in range(LRU_TILES):
                    a_ref[dr, j, row:row + hi - lo, :] = a[lo:hi, j * LANE:(j + 1) * LANE]
                    u_ref[dr, j, row:row + hi - lo, :] = u[lo:hi, j * LANE:(j + 1) * LANE]


def _lru_scan(n, a_ref, u_ref, hs_ref, h0):
    seg = n // LRU_SEGS
    pitch = _lru_pitch(n)
    chains = [(d, j) for d in range(2) for j in range(LRU_TILES)]

    def rows(d, t):
        return pl.ds(t if d == 0 else seg - 1 - t, LRU_SEGS, stride=pitch)

    def sweep_transfer(t, carry):
        out = []
        for (d, j), (f, p) in zip(chains, carry):
            a = a_ref[d, j, rows(d, t), :]
            out.append((a * f + u_ref[d, j, rows(d, t), :], a * p))
        return tuple(out)

    zeros = jnp.zeros((LRU_SEGS, LANE), F32)
    ones = jnp.ones((LRU_SEGS, LANE), F32)
    transfer = lax.fori_loop(0, seg, sweep_transfer, tuple((zeros, ones) for _ in chains), unroll=LRU_SCAN_UNROLL)

    entries, exits = [], []
    for (d, j), (f, p) in zip(chains, transfer):
        state = h0[d][j]
        entry = [None] * LRU_SEGS
        for k in (range(LRU_SEGS) if d == 0 else range(LRU_SEGS - 1, -1, -1)):
            entry[k] = state
            state = p[k:k + 1, :] * state + f[k:k + 1, :]
        entries.append(jnp.concatenate(entry, axis=0))
        exits.append(state)

    def sweep_states(t, carry):
        out = []
        for (d, j), h in zip(chains, carry):
            h = a_ref[d, j, rows(d, t), :] * h + u_ref[d, j, rows(d, t), :]
            hs_ref[d, j, rows(d, t), :] = h
            out.append(h)
        return tuple(out)

    lax.fori_loop(0, seg, sweep_states, tuple(entries), unroll=LRU_SCAN_UNROLL)
    return [[exits[d * LRU_TILES + j] for j in range(LRU_TILES)] for d in range(2)]


def _lru_output(n, hs_ref, lg_ref, o_ref):
    for c0 in range(0, n, LRU_CHUNK):
        cn = min(LRU_CHUNK, n - c0)
        parts = []
        for lo, hi, row in _lru_pieces(c0, cn, n):
            parts.append(jnp.concatenate(
                [hs_ref[0, j, row:row + hi - lo, :] + hs_ref[1, j, row:row + hi - lo, :] for j in range(LRU_TILES)],
                axis=-1))
        hsum = parts[0] if len(parts) == 1 else jnp.concatenate(parts, axis=0)
        o_ref[c0:c0 + cn, :] = (_gelu_tanh(lg_ref[c0:c0 + cn, :]) * hsum).astype(BF)


def _lru_kernel(lx_ref, lg_ref, clx_ref, clg_ref, cw_ref, cb_ref, wg_ref, bg_ref, lam_ref,
                o_ref, oc_ref, xpad_ref, a_ref, u_ref, hs_ref, ac_ref, uc_ref):
    n = lx_ref.shape[0]
    nc = clx_ref.shape[0]
    nlam = -lam_ref[...]
    sp = jnp.maximum(nlam, 0.0) + jnp.log(1.0 + jnp.exp(-jnp.abs(nlam)))
    sp2 = (-0.5 * LRU_C * LOG2E) * sp
    zero = jnp.zeros((1, LANE), F32)
    _lru_coeffs(clx_ref, nc, xpad_ref, ac_ref, uc_ref, cw_ref, cb_ref, wg_ref, bg_ref, sp2)
    h_ctx = _lru_scan(nc, ac_ref, uc_ref, hs_ref, [[zero] * LRU_TILES] * 2)
    _lru_output(nc, hs_ref, clg_ref, oc_ref)
    _lru_coeffs(lx_ref, n, xpad_ref, a_ref, u_ref, cw_ref, cb_ref, wg_ref, bg_ref, sp2)
    _lru_scan(n, a_ref, u_ref, hs_ref, h_ctx)
    _lru_output(n, hs_ref, lg_ref, o_ref)


def _lru_call(lx, lg, clx, clg, l, wts, batch):
    n, w = lx.shape
    s = n // batch
    c = clx.shape[0] // batch
    seq = lambda a, ln: a.reshape(batch, ln, w)
    lat = pl.BlockSpec((None, s, w), lambda b: (b, 0, 0))
    ctx = pl.BlockSpec((None, c, w), lambda b: (b, 0, 0))
    stacked = [wts["conv_w"], wts["conv_b"], wts["w_gate"], wts["b_gate"], wts["lam"]]
    o, oc = pl.pallas_call(
        _lru_kernel,
        grid=(batch,),
        in_specs=[lat, lat, ctx, ctx] + [_layer_spec(a, l) for a in stacked],
        out_specs=[lat, ctx],
        out_shape=[jax.ShapeDtypeStruct((batch, s, w), BF), jax.ShapeDtypeStruct((batch, c, w), BF)],
        scratch_shapes=[pltpu.VMEM((s + 2 * LRU_PAD, w), F32)]
                       + [pltpu.VMEM((2, LRU_TILES, LRU_SEGS * _lru_pitch(ln), LANE), F32) for ln in (s, s, s, c, c)],
        compiler_params=_params("parallel"),
        name="rglru",
    )(seq(lx, s), seq(lg, s), seq(clx, c), seq(clg, c), *stacked)
    return o.reshape(n, w), oc.reshape(batch * c, w)


def _out_kernel(h_ref, na_ref, mla_ref, lru_ref, gm_ref, shf_ref, scf_ref, gf_ref, g_ref,
                wo_ref, wg_ref, wu_ref, wd_ref, o_ref, *, hid_chunk, sub_rows):
    hidden = wg_ref.shape[1]

    def residual_and_norm(r0):
        rs = slice(r0, r0 + sub_rows)
        o_cat = jnp.concatenate([na_ref[rs, :], mla_ref[rs, :], lru_ref[rs, :]], axis=-1)
        h1 = h_ref[rs, :] + gm_ref[...] * _dot(o_cat, wo_ref[...])
        y = h1 * lax.rsqrt(jnp.mean(h1 * h1, axis=-1, keepdims=True) + EPS) * g_ref[...]
        return h1, (y * (1.0 + scf_ref[...]) + shf_ref[...]).astype(BF)

    def gate_up(y, c0):
        return _dot(y, wg_ref[:, c0:c0 + hid_chunk]), _dot(y, wu_ref[:, c0:c0 + hid_chunk])

    starts = list(range(0, h_ref.shape[0], sub_rows))
    h1, y = residual_and_norm(starts[0])
    nxt = gate_up(y, 0)
    for si, r0 in enumerate(starts):
        acc = None
        h1_next = y_next = None
        for c0 in range(0, hidden, hid_chunk):
            gate, up = nxt
            if c0 + hid_chunk < hidden:
                nxt = gate_up(y, c0 + hid_chunk)
            elif si + 1 < len(starts):
                h1_next, y_next = residual_and_norm(starts[si + 1])
                nxt = gate_up(y_next, 0)
            d = _dot((_silu(gate) * up).astype(BF), wd_ref[c0:c0 + hid_chunk, :])
            acc = d if acc is None else acc + d
        o_ref[r0:r0 + sub_rows, :] = h1 + gf_ref[...] * acc
        h1, y = h1_next, y_next


def _out_call(h, na_o, mla_o, lru_o, mods, row_map, l, wts, tm, sub_rows):
    n, d = h.shape
    row = lambda w: pl.BlockSpec((tm, w), lambda i: (i, 0))
    stacked = [wts["g_ffn"], wts["w_out"], wts["ffn_gate"], wts["ffn_up"], wts["ffn_down"]]
    hidden = wts["ffn_gate"].shape[-1]
    hid_chunk = 2 * LANE if hidden % (2 * LANE) == 0 else hidden
    return pl.pallas_call(
        functools.partial(_out_kernel, hid_chunk=hid_chunk, sub_rows=sub_rows),
        grid=(n // tm,),
        in_specs=[row(d), row(na_o.shape[1]), row(mla_o.shape[1]), row(lru_o.shape[1])]
                 + [_mod_spec(mods, l, which, row_map) for which in (2, 3, 4, 5)]
                 + [_layer_spec(a, l) for a in stacked],
        out_specs=row(d),
        out_shape=jax.ShapeDtypeStruct((n, d), F32),
        compiler_params=_params("parallel"),
        name="out_ffn",
    )(h, na_o, mla_o, lru_o, mods, mods, mods, mods, *stacked)


def _rope_perm():
    j = np.arange(MLA_ROPE)
    first = (j % (MLA_ROPE // 2)) < MLA_ROPE // 4
    partner = np.where(first, j + MLA_ROPE // 4, j - MLA_ROPE // 4)
    sign = np.where(first, -1.0, 1.0).astype(np.float32)
    return partner, sign


def _rope_tables(s):
    t = np.arange(s)
    row = (t // GRID_W).astype(np.float32)
    col = (t % GRID_W).astype(np.float32)
    n_freq = MLA_ROPE // 4
    inv = jnp.asarray(ROPE_BASE, F32) ** (-jnp.arange(n_freq, dtype=F32) / n_freq)
    ar = jnp.asarray(row)[:, None] * inv
    ac = jnp.asarray(col)[:, None] * inv
    ang = jnp.concatenate([ar, ar, ac, ac], axis=-1)
    pad = LANE - MLA_NOPE - MLA_ROPE
    cos = jnp.concatenate([jnp.ones((s, MLA_NOPE), F32), jnp.cos(ang), jnp.zeros((s, pad), F32)], axis=-1)
    sin = jnp.concatenate([jnp.zeros((s, MLA_NOPE), F32), jnp.sin(ang), jnp.zeros((s, pad), F32)], axis=-1)
    return cos, sin


def _lane_tile(nope, rope):
    lead = nope.shape[:-1]
    pad = jnp.zeros(lead + (LANE - MLA_NOPE - MLA_ROPE,), nope.dtype)
    return jnp.concatenate([nope, rope, pad], axis=-1)


def _prep_weights(p):
    partner, sign = _rope_perm()
    w_in = p["w_in"]
    depth, d, _ = w_in.shape
    cols = np.cumsum((NA_WIDTH, NA_WIDTH, NA_WIDTH, MLA_Q_RANK, MLA_KV_RANK, MLA_ROPE, LRU_WIDTH, LRU_WIDTH))
    w_kr = w_in[..., cols[4]:cols[5]]
    zn = jnp.zeros((depth, d, MLA_NOPE), F32)
    w_all = jnp.concatenate([
        w_in[..., :cols[4]], _lane_tile(zn, w_kr), _lane_tile(zn, w_kr[..., partner]), w_in[..., cols[5]:]], axis=-1)
    assert w_all.shape[-1] == W_ALL

    wq = p["mla_w_qb"].reshape(depth, MLA_Q_RANK, MLA_HEADS, MLA_NOPE + MLA_ROPE)
    wq_n, wq_r = wq[..., :MLA_NOPE], wq[..., MLA_NOPE:]
    w_qb = jnp.concatenate([
        _lane_tile(wq_n, wq_r).reshape(depth, MLA_Q_RANK, MLA_QK_W),
        _lane_tile(jnp.zeros_like(wq_n), wq_r[..., partner]).reshape(depth, MLA_Q_RANK, MLA_QK_W)], axis=-1)

    wkv = p["mla_w_kvb"].reshape(depth, MLA_KV_RANK, MLA_HEADS, MLA_NOPE + MLA_V)
    wk_n, wv = wkv[..., :MLA_NOPE], wkv[..., MLA_NOPE:]
    w_kvb = jnp.concatenate([
        _lane_tile(wk_n, jnp.zeros(wk_n.shape[:-1] + (MLA_ROPE,), F32)).reshape(depth, MLA_KV_RANK, MLA_QK_W),
        wv.reshape(depth, MLA_KV_RANK, MLA_HEADS * MLA_V)], axis=-1)

    qg, kg = p["mla_q_g"], p["mla_k_g"]
    z_n, z_r = jnp.zeros((depth, MLA_NOPE), F32), jnp.zeros((depth, MLA_ROPE), F32)
    sgn = jnp.asarray(sign)
    zrow = jnp.zeros((depth, LANE), F32)
    g_mla = jnp.stack([
        _lane_tile(qg[:, :MLA_NOPE], qg[:, MLA_NOPE:]) * MLA_SCALE,
        _lane_tile(z_n, sgn * qg[:, MLA_NOPE:][:, partner]) * MLA_SCALE,
        _lane_tile(kg[:, :MLA_NOPE], z_r),
        _lane_tile(z_n, kg[:, MLA_NOPE:]),
        _lane_tile(z_n, sgn * kg[:, MLA_NOPE:][:, partner]),
        zrow, zrow, zrow], axis=1)

    g_na = jnp.concatenate([jnp.tile(p["na_q_g"], (1, NA_HEADS)) * NA_SCALE, jnp.tile(p["na_k_g"], (1, NA_HEADS))],
                           axis=-1)[:, None]

    wa, wx = p["lru_w_a"], p["lru_w_x"]
    wg4 = jnp.stack([wa[:, 0], wa[:, 1], wx[:, 0], wx[:, 1]], axis=1)
    eye = jnp.eye(LRU_HEADS, dtype=F32)
    w_gate = (wg4[:, :, :, :, None, :] * eye[None, None, :, None, :, None])
    w_gate = w_gate.transpose(0, 2, 3, 1, 4, 5).reshape(depth, LRU_WIDTH, 4 * LRU_WIDTH)
    ba, bx = p["lru_b_a"], p["lru_b_x"]
    b_gate = jnp.concatenate([ba[:, 0], ba[:, 1], bx[:, 0], bx[:, 1]], axis=-1)[:, None]

    return {
        "g_mix": p["norm_mix_g"][:, None], "g_ffn": p["norm_ffn_g"][:, None],
        "w_all": w_all.astype(BF), "w_qb": w_qb.astype(BF), "w_kvb": w_kvb.astype(BF),
        "g_na": g_na, "g_cq": p["mla_cq_g"][:, None], "g_ckv": p["mla_ckv_g"][:, None], "g_mla": g_mla,
        "conv_w": p["lru_conv_w"], "conv_b": p["lru_conv_b"][:, None],
        "w_gate": (0.5 * w_gate).astype(BF), "b_gate": 0.5 * b_gate, "lam": p["lru_lambda"].reshape(depth, 1, 2 * LRU_WIDTH),
        "w_out": p["w_out"].astype(BF), "ffn_gate": p["ffn_w_gate"].astype(BF),
        "ffn_up": p["ffn_w_up"].astype(BF), "ffn_down": p["ffn_w_down"].astype(BF),
    }


def kernel(x, c, ctx, c_ctx, ada_w, ada_b, norm_mix_g, norm_ffn_g, w_in, na_q_g, na_k_g, na_rpb, mla_cq_g, mla_w_qb, mla_ckv_g, mla_w_kvb, mla_q_g, mla_k_g, lru_conv_w, lru_conv_b, lru_w_a, lru_b_a, lru_w_x, lru_b_x, lru_lambda, w_out, ffn_w_gate, ffn_w_up, ffn_w_down):
    batch, s, d = x.shape
    cl = ctx.shape[1]
    depth = ada_w.shape[0]
    rows = s // GRID_W
    assert s % GRID_W == 0 and rows % (NA_BLOCKS_PER_STEP * NA_QROWS) == 0 and rows >= 2 * NA_KROWS
    p = dict(norm_mix_g=norm_mix_g, norm_ffn_g=norm_ffn_g, w_in=w_in, na_q_g=na_q_g, na_k_g=na_k_g,
             mla_cq_g=mla_cq_g, mla_w_qb=mla_w_qb, mla_ckv_g=mla_ckv_g, mla_w_kvb=mla_w_kvb, mla_q_g=mla_q_g,
             mla_k_g=mla_k_g, lru_conv_w=lru_conv_w, lru_conv_b=lru_conv_b, lru_w_a=lru_w_a, lru_b_a=lru_b_a,
             lru_w_x=lru_w_x, lru_b_x=lru_b_x, lru_lambda=lru_lambda, w_out=w_out, ffn_w_gate=ffn_w_gate,
             ffn_w_up=ffn_w_up, ffn_w_down=ffn_w_down)

    sub = 512 if s % 512 == 0 else GRID_W * NA_QROWS
    tm = 2 * sub if s % (2 * sub) == 0 else sub
    subc = sub if (batch * cl) % sub == 0 else cl
    tmc = 2 * subc if (batch * cl) % (2 * subc) == 0 else subc
    tq = 512

    nmod = -(-(batch + 1) // SUBLANE) * SUBLANE
    cvec = jnp.concatenate([c, c_ctx[None], jnp.zeros((nmod - batch - 1, d), F32)], axis=0)
    mods = _ada_call(cvec, ada_w, ada_b).reshape(depth, nmod, 6, 1, d)
    tiles_per_batch = s // tm
    lat_row = lambda i: i // tiles_per_batch
    ctx_row = lambda i: batch

    wts = _prep_weights(p)
    na_bias = _na_bias_tables(na_rpb, rows)
    cos, sin = _rope_tables(s)
    cos_c = jnp.concatenate([jnp.ones((tmc, MLA_NOPE + MLA_ROPE), F32),
                             jnp.zeros((tmc, LANE - MLA_NOPE - MLA_ROPE), F32)], axis=-1)
    sin_c = jnp.zeros((tmc, LANE), F32)

    h = x.reshape(batch * s, d)
    hc = ctx.reshape(batch * cl, d)
    for l in range(depth):
        last = l == depth - 1
        q, k, v, mq, mk, mv, lx, lg = _in_call(h, mods, lat_row, l, wts, cos, sin, tm, sub // 2)
        cq_, ck, cv, cmq, cmk, cmv, clx, clg = _in_call(hc, mods, ctx_row, l, wts, cos_c, sin_c, tmc, subc // 2)

        na_o = _na_call(q, k, v, ck, cv, na_bias, l, batch)
        mla_o = _mla_call(mq, mk, mv, cmk, cmv, batch, tq)
        lru_o, lru_c = _lru_call(lx, lg, clx, clg, l, wts, batch)
        h = _out_call(h, na_o, mla_o, lru_o, mods, lat_row, l, wts, tm, sub)

        if not last:
            na_c, mla_c = _ctx_attn_call(cq_, ck, cv, cmq, cmk, cmv, batch)
            hc = _out_call(hc, na_c, mla_c, lru_c, mods, ctx_row, l, wts, tmc, subc)
    return h.reshape(batch, s, d)
```

```python
import functools
import math

import numpy as np
import jax
import jax.numpy as jnp
from jax import lax
from jax.experimental import pallas as pl
from jax.experimental.pallas import tpu as pltpu

GRID_W = 64
NA_HEADS = 6
NA_HEAD_DIM = 64
NA_WIDTH = NA_HEADS * NA_HEAD_DIM
NA_WIN_H = 8
NA_WIN_W = 16
MLA_HEADS = 6
MLA_Q_RANK = 256
MLA_KV_RANK = 128
MLA_NOPE = 64
MLA_ROPE = 32
MLA_V = 64
LRU_WIDTH = 256
LRU_HEADS = 4
LRU_BLOCK = LRU_WIDTH // LRU_HEADS
LRU_CONV_W = 4
LRU_C = 8.0
ROPE_BASE = 10000.0
EPS = 1e-6
LOG2E = math.log2(math.e)
NA_SCALE = NA_HEAD_DIM ** -0.5 * LOG2E
MLA_SCALE = (MLA_NOPE + MLA_ROPE) ** -0.5 * LOG2E

LANE = 128
SUBLANE = 8
MLA_QK_W = MLA_HEADS * LANE
NEG_BIG = -1e30
VMEM_LIMIT = 56 * 1024 * 1024

NA_QROWS = 4
NA_KROWS = NA_QROWS + NA_WIN_H
NA_CLASSES = 3
NA_BLOCKS_PER_STEP = 4

C_Q, C_K, C_V = 0, NA_WIDTH, 2 * NA_WIDTH
C_CQ = 3 * NA_WIDTH
C_CKV = C_CQ + MLA_Q_RANK
C_KR = C_CKV + MLA_KV_RANK
C_KRP = C_KR + LANE
C_LX = C_KRP + LANE
C_LG = C_LX + LRU_WIDTH
W_ALL = C_LG + LRU_WIDTH

BF = jnp.bfloat16
F32 = jnp.float32


def _dot(a, b):
    return jnp.dot(a, b, preferred_element_type=F32)


def _dot_nt(a, b):
    return lax.dot_general(a, b, (((1,), (1,)), ((), ())), preferred_element_type=F32)


def _silu(x):
    hx = 0.5 * x
    return hx + hx * jnp.tanh(hx)


def _gelu_tanh(x):
    return 0.5 * x * (1.0 + jnp.tanh(0.7978845608028654 * (x + 0.044715 * (x * x * x))))


def _params(*sem):
    return pltpu.CompilerParams(dimension_semantics=sem, vmem_limit_bytes=VMEM_LIMIT)


def _const_spec(shape):
    nd = len(shape)
    return pl.BlockSpec(shape, lambda *_: (0,) * nd, pipeline_mode=pl.Buffered(1))


def _layer_spec(arr, l):
    nd = arr.ndim
    return pl.BlockSpec((None,) + arr.shape[1:], lambda *_: (l,) + (0,) * (nd - 1), pipeline_mode=pl.Buffered(1))


def _mod_spec(mods, l, which, row_map):
    d = mods.shape[-1]
    return pl.BlockSpec((None, None, None, 1, d), lambda i: (l, row_map(i), which, 0, 0))


def _ada_kernel(c_ref, w_ref, b_ref, o_ref):
    s = _silu(c_ref[...]).astype(BF)
    o_ref[...] = _dot(s, w_ref[...].astype(BF)) + b_ref[...]


def _ada_call(cvec, ada_w, ada_b):
    depth, d, d6 = ada_w.shape
    r = cvec.shape[0]
    tn = 1024
    return pl.pallas_call(
        _ada_kernel,
        grid=(depth, d6 // tn),
        in_specs=[
            pl.BlockSpec((r, d), lambda l, j: (0, 0)),
            pl.BlockSpec((None, d, tn), lambda l, j: (l, 0, j)),
            pl.BlockSpec((None, 1, tn), lambda l, j: (l, 0, j)),
        ],
        out_specs=pl.BlockSpec((None, r, tn), lambda l, j: (l, 0, j)),
        out_shape=jax.ShapeDtypeStruct((depth, r, d6), F32),
        compiler_params=_params("parallel", "parallel"),
        name="adaln",
    )(cvec, ada_w, ada_b.reshape(depth, 1, d6))


def _in_kernel(h_ref, sh_ref, sc_ref, g_ref, wall_ref, wqb_ref, wkvb_ref, gna_ref, gcq_ref, gckv_ref,
               gm_ref, cos_ref, sin_ref,
               q_ref, k_ref, v_ref, mq_ref, mk_ref, mv_ref, lx_ref, lg_ref, *, sub_rows):
    lane = lax.broadcasted_iota(jnp.int32, (sub_rows, LANE), 1)
    na_first = lane < NA_HEAD_DIM
    nope = lane < MLA_NOPE

    def seg_rsqrt(t, in_first, n_first, n_second):
        t2 = t * t
        s1 = jnp.sum(jnp.where(in_first, t2, 0.0), axis=-1, keepdims=True) * (1.0 / n_first)
        s2 = jnp.sum(jnp.where(in_first, 0.0, t2), axis=-1, keepdims=True) * (1.0 / n_second)
        return lax.rsqrt(jnp.where(in_first, s1, s2) + EPS)

    def store_values(ref, rs, vals):
        for p in range(vals.shape[1] // LANE):
            vp = vals[:, p * LANE:(p + 1) * LANE]
            ref[rs, (2 * p) * LANE:(2 * p + 1) * LANE] = jnp.where(na_first, vp, 1.0).astype(BF)
            ref[rs, (2 * p + 1) * LANE:(2 * p + 2) * LANE] = jnp.where(na_first, 1.0, vp).astype(BF)

    def project(r0):
        x = h_ref[r0:r0 + sub_rows, :]
        y = x * lax.rsqrt(jnp.mean(x * x, axis=-1, keepdims=True) + EPS) * g_ref[...]
        y = y * (1.0 + sc_ref[...]) + sh_ref[...]
        return _dot(y.astype(BF), wall_ref[...])

    def head_groups(r0, u):
        rs = slice(r0, r0 + sub_rows)
        for p in range(2 * NA_WIDTH // LANE):
            lo = p * LANE
            t = u[:, C_Q + lo:C_Q + lo + LANE]
            tn = (t * seg_rsqrt(t, na_first, NA_HEAD_DIM, NA_HEAD_DIM) * gna_ref[:, lo:lo + LANE]).astype(BF)
            if lo < NA_WIDTH:
                q_ref[rs, lo:lo + LANE] = tn
            else:
                k_ref[rs, lo - NA_WIDTH:lo - NA_WIDTH + LANE] = tn
        store_values(v_ref, rs, u[:, C_V:C_V + NA_WIDTH])

        cos = cos_ref[rs, :]
        sin = sin_ref[rs, :]
        cq = u[:, C_CQ:C_CQ + MLA_Q_RANK]
        cqn = cq * lax.rsqrt(jnp.mean(cq * cq, axis=-1, keepdims=True) + EPS) * gcq_ref[...]
        qq = _dot(cqn.astype(BF), wqb_ref[...])
        a_q = gm_ref[0:1, :] * cos
        b_q = gm_ref[1:2, :] * sin
        for hd in range(MLA_HEADS):
            lo = hd * LANE
            qr = qq[:, lo:lo + LANE]
            qp = qq[:, MLA_QK_W + lo:MLA_QK_W + lo + LANE]
            mq_ref[rs, lo:lo + LANE] = (seg_rsqrt(qr, nope, MLA_NOPE, MLA_ROPE) * (qr * a_q + qp * b_q)).astype(BF)

        ckv = u[:, C_CKV:C_CKV + MLA_KV_RANK]
        ckvn = ckv * lax.rsqrt(jnp.mean(ckv * ckv, axis=-1, keepdims=True) + EPS) * gckv_ref[...]
        kv = _dot(ckvn.astype(BF), wkvb_ref[...])
        krr = u[:, C_KR:C_KR + LANE]
        krp = u[:, C_KRP:C_KRP + LANE]
        msr = jnp.sum(krr * krr, axis=-1, keepdims=True) * (1.0 / MLA_ROPE)
        kr = lax.rsqrt(msr + EPS) * (krr * (gm_ref[3:4, :] * cos) + krp * (gm_ref[4:5, :] * sin))
        for hd in range(MLA_HEADS):
            lo = hd * LANE
            kn = kv[:, lo:lo + LANE]
            ms = jnp.sum(kn * kn, axis=-1, keepdims=True) * (1.0 / MLA_NOPE)
            mk_ref[rs, lo:lo + LANE] = (kn * lax.rsqrt(ms + EPS) * gm_ref[2:3, :] + kr).astype(BF)
        store_values(mv_ref, rs, kv[:, MLA_QK_W:])

        lx_ref[rs, :] = u[:, C_LX:C_LX + LRU_WIDTH]
        lg_ref[rs, :] = u[:, C_LG:C_LG + LRU_WIDTH]

    starts = list(range(0, h_ref.shape[0], sub_rows))
    u_next = project(starts[0])
    for si, r0 in enumerate(starts):
        u = u_next
        if si + 1 < len(starts):
            u_next = project(starts[si + 1])
        head_groups(r0, u)


def _in_call(h, mods, row_map, l, wts, cos, sin, tm, sub_rows):
    n, d = h.shape
    tab_tiles = cos.shape[0] // tm
    row = lambda w: pl.BlockSpec((tm, w), lambda i: (i, 0))
    tab = pl.BlockSpec((tm, LANE), lambda i: (i % tab_tiles, 0))
    stacked = [wts["g_mix"], wts["w_all"], wts["w_qb"], wts["w_kvb"], wts["g_na"], wts["g_cq"], wts["g_ckv"],
               wts["g_mla"]]
    out_w = [(NA_WIDTH, BF), (NA_WIDTH, BF), (NA_HEADS * LANE, BF), (MLA_QK_W, BF), (MLA_QK_W, BF),
             (MLA_HEADS * LANE, BF), (LRU_WIDTH, F32), (LRU_WIDTH, F32)]
    return pl.pallas_call(
        functools.partial(_in_kernel, sub_rows=sub_rows),
        grid=(n // tm,),
        in_specs=[row(d), _mod_spec(mods, l, 0, row_map), _mod_spec(mods, l, 1, row_map)]
                 + [_layer_spec(a, l) for a in stacked] + [tab, tab],
        out_specs=[row(w) for w, _ in out_w],
        out_shape=[jax.ShapeDtypeStruct((n, w), dt) for w, dt in out_w],
        compiler_params=_params("parallel"),
        name="in_proj",
    )(h, mods, mods, *stacked, cos, sin)


def _attention_pipeline(work, scores, values, emit):
    s_next = scores(work[0])
    for i, item in enumerate(work):
        s_cur = s_next
        if i + 1 < len(work):
            s_next = scores(work[i + 1])
        m = functools.reduce(jnp.maximum, [jnp.max(s, axis=-1, keepdims=True) for s in s_cur])
        o = sum(_dot(jnp.exp2(s - m).astype(BF), v) for s, v in zip(s_cur, values(item)))
        emit(item, o / pltpu.roll(o, LANE // 2, 1))


def _na_block_geometry(rows):
    nblk = rows // NA_QROWS
    half = NA_WIN_H // 2
    ks = np.clip(np.arange(nblk) * NA_QROWS - half, 0, rows - NA_KROWS)
    pats = []
    for j in range(nblk):
        r = j * NA_QROWS + np.arange(NA_QROWS)
        rs = np.clip(r - half, 0, rows - NA_WIN_H)
        assert ks[j] <= rs.min() and rs.max() + NA_WIN_H <= ks[j] + NA_KROWS
        pats.append(tuple(zip((r - ks[j]).tolist(), (rs - ks[j]).tolist())))
    cls = [0 if j == 0 else (2 if j == nblk - 1 else 1) for j in range(nblk)]
    by_cls = {}
    for j in range(nblk):
        assert by_cls.setdefault(cls[j], pats[j]) == pats[j]
    return [by_cls[c] for c in range(NA_CLASSES)]


def _na_kernel(q_ref, k_ref, v_ref, kc_ref, vc_ref, bias_ref, o_ref, *, rows, layout):
    nblk = rows // NA_QROWS
    nq = NA_QROWS * GRID_W
    nk = NA_KROWS * GRID_W
    first_half = lax.broadcasted_iota(jnp.int32, (nq, LANE), 1) < NA_HEAD_DIM

    def block_geometry(sub):
        j = pl.program_id(1) * NA_BLOCKS_PER_STEP + sub
        ks = jnp.clip(j * NA_QROWS - NA_WIN_H // 2, 0, rows - NA_KROWS)

        def piece(i, t):
            top, mid, bot = (layout[c][i][t] for c in range(NA_CLASSES))
            if top == mid == bot:
                return mid
            return jnp.where(j == 0, top, jnp.where(j == nblk - 1, bot, mid))

        pieces = [[piece(i, t) for t in range(NA_KROWS // 2)] for i in range(NA_QROWS)]
        return pl.multiple_of(ks * GRID_W, GRID_W), pieces

    geom = [block_geometry(sub) for sub in range(NA_BLOCKS_PER_STEP)]

    def scores(item):
        sub, hd = item
        start, pieces = geom[sub]
        lo = (hd // 2) * LANE
        q2 = q_ref[sub * nq:(sub + 1) * nq, lo:lo + LANE]
        qm = jnp.where(first_half if hd % 2 == 0 else ~first_half, q2, jnp.zeros_like(q2))
        bias = jnp.concatenate([jnp.concatenate([bias_ref[hd, pid] for pid in row], axis=-1) for row in pieces], axis=0)
        s_w = _dot_nt(qm, k_ref[pl.ds(start, nk), lo:lo + LANE]) + bias
        return s_w, _dot_nt(qm, kc_ref[:, lo:lo + LANE])

    def values(item):
        sub, hd = item
        return v_ref[pl.ds(geom[sub][0], nk), hd * LANE:(hd + 1) * LANE], vc_ref[:, hd * LANE:(hd + 1) * LANE]

    even_head = {}

    def emit(item, out):
        sub, hd = item
        if hd % 2 == 0:
            even_head[sub] = out
        else:
            lo = (hd // 2) * LANE
            o_ref[sub * nq:(sub + 1) * nq, lo:lo + LANE] = jnp.where(first_half, even_head[sub], out).astype(BF)

    _attention_pipeline([(sub, hd) for sub in range(NA_BLOCKS_PER_STEP) for hd in range(NA_HEADS)],
                        scores, values, emit)


def _na_call(q, k, v, kc, vc, bias, l, batch):
    n, w = q.shape
    s = n // batch
    c = kc.shape[0] // batch
    rows = s // GRID_W
    wv = v.shape[1]
    q3, k3, v3 = q.reshape(batch, s, w), k.reshape(batch, s, w), v.reshape(batch, s, wv)
    kc3, vc3 = kc.reshape(batch, c, w), vc.reshape(batch, c, wv)
    nq = NA_BLOCKS_PER_STEP * NA_QROWS * GRID_W
    out = pl.pallas_call(
        functools.partial(_na_kernel, rows=rows, layout=_na_bias_layout(rows)[1]),
        grid=(batch, rows // (NA_BLOCKS_PER_STEP * NA_QROWS)),
        in_specs=[
            pl.BlockSpec((None, nq, w), lambda b, j: (b, j, 0)),
            pl.BlockSpec((None, s, w), lambda b, j: (b, 0, 0)),
            pl.BlockSpec((None, s, wv), lambda b, j: (b, 0, 0)),
            pl.BlockSpec((None, c, w), lambda b, j: (b, 0, 0)),
            pl.BlockSpec((None, c, wv), lambda b, j: (b, 0, 0)),
            _layer_spec(bias, l),
        ],
        out_specs=pl.BlockSpec((None, nq, w), lambda b, j: (b, j, 0)),
        out_shape=jax.ShapeDtypeStruct((batch, s, w), BF),
        compiler_params=_params("parallel", "arbitrary"),
        name="nbr_attn",
    )(q3, k3, v3, kc3, vc3, bias)
    return out.reshape(n, w)


def _na_bias_layout(rows):
    n_dr = 2 * NA_WIN_H - 1
    pair_ids = {}
    layout = []
    for pat in _na_block_geometry(rows):
        layout.append([])
        for qo, wo in pat:
            dr = [kr - qo + NA_WIN_H - 1 if wo <= kr < wo + NA_WIN_H else n_dr for kr in range(NA_KROWS)]
            layout[-1].append([pair_ids.setdefault((dr[2 * j], dr[2 * j + 1]), len(pair_ids))
                               for j in range(NA_KROWS // 2)])
    return list(pair_ids), layout


def _na_bias_pieces(rpb, rows):
    depth = rpb.shape[0]
    qc = np.arange(GRID_W)
    kcol = np.arange(GRID_W)
    cs = np.clip(qc - NA_WIN_W // 2, 0, GRID_W - NA_WIN_W)
    ok = (kcol[None, :] >= cs[:, None]) & (kcol[None, :] < cs[:, None] + NA_WIN_W)
    col_rel = np.clip(kcol[None, :] - qc[:, None], 1 - NA_WIN_W, NA_WIN_W - 1) + NA_WIN_W - 1
    sel = (col_rel[None] == np.arange(2 * NA_WIN_W - 1)[:, None, None]).astype(np.float32)
    t = jnp.einsum("lhrd,dqk->lhrqk", rpb.astype(F32), jnp.asarray(sel), precision=lax.Precision.HIGHEST)
    t = jnp.where(ok, t * LOG2E, NEG_BIG)
    neg = jnp.full((depth, NA_HEADS, 1, GRID_W, GRID_W), NEG_BIG, F32)
    t_ext = jnp.concatenate([t, neg], axis=2)
    pieces, _ = _na_bias_layout(rows)
    return jnp.stack([jnp.concatenate([t_ext[:, :, a], t_ext[:, :, b]], axis=-1) for a, b in pieces], axis=2)


MLA_TILES_PER_STEP = 2


def _mla_kernel(q_ref, k_ref, v_ref, kc_ref, vc_ref, o_ref, *, tq):
    first_half = lax.broadcasted_iota(jnp.int32, (tq, LANE), 1) < MLA_V

    def scores(item):
        sub, hd = item
        hl = hd * LANE
        qh = q_ref[sub * tq:(sub + 1) * tq, hl:hl + LANE]
        return _dot_nt(qh, k_ref[:, hl:hl + LANE]), _dot_nt(qh, kc_ref[:, hl:hl + LANE])

    def values(item):
        hl = item[1] * LANE
        return v_ref[:, hl:hl + LANE], vc_ref[:, hl:hl + LANE]

    even_head = {}

    def emit(item, out):
        sub, hd = item
        if hd % 2 == 0:
            even_head[sub] = out
        else:
            lo = (hd // 2) * LANE
            o_ref[sub * tq:(sub + 1) * tq, lo:lo + LANE] = jnp.where(first_half, even_head[sub], out).astype(BF)

    _attention_pipeline([(sub, hd) for sub in range(q_ref.shape[0] // tq) for hd in range(MLA_HEADS)],
                        scores, values, emit)


def _mla_call(q, k, v, kc, vc, batch, tq):
    n, wq = q.shape
    wv = v.shape[1]
    wo = MLA_HEADS * MLA_V
    s = n // batch
    c = kc.shape[0] // batch
    q3, k3 = q.reshape(batch, s, wq), k.reshape(batch, s, wq)
    v3 = v.reshape(batch, s, wv)
    kc3, vc3 = kc.reshape(batch, c, wq), vc.reshape(batch, c, wv)
    step_q = MLA_TILES_PER_STEP * tq
    out = pl.pallas_call(
        functools.partial(_mla_kernel, tq=tq),
        grid=(batch, s // step_q),
        in_specs=[
            pl.BlockSpec((None, step_q, wq), lambda b, i: (b, i, 0)),
            pl.BlockSpec((None, s, wq), lambda b, i: (b, 0, 0)),
            pl.BlockSpec((None, s, wv), lambda b, i: (b, 0, 0)),
            pl.BlockSpec((None, c, wq), lambda b, i: (b, 0, 0)),
            pl.BlockSpec((None, c, wv), lambda b, i: (b, 0, 0)),
        ],
        out_specs=pl.BlockSpec((None, step_q, wo), lambda b, i: (b, i, 0)),
        out_shape=jax.ShapeDtypeStruct((batch, s, wo), BF),
        compiler_params=_params("parallel", "arbitrary"),
        name="latent_attn",
    )(q3, k3, v3, kc3, vc3)
    return out.reshape(n, wo)


def _ctx_attn_kernel(q_ref, k_ref, v_ref, mq_ref, mk_ref, mv_ref, ona_ref, omla_ref):
    c = q_ref.shape[0]
    first_half = lax.broadcasted_iota(jnp.int32, (c, LANE), 1) < NA_HEAD_DIM

    def attend(s, v):
        m = jnp.max(s, axis=-1, keepdims=True)
        p = jnp.exp2(s - m)
        return _dot(p.astype(BF), v) / jnp.sum(p, axis=-1, keepdims=True)

    def value_pair(ref, p):
        return jnp.where(first_half, ref[:, (2 * p) * LANE:(2 * p + 1) * LANE],
                         ref[:, (2 * p + 1) * LANE:(2 * p + 2) * LANE])

    for p in range(NA_WIDTH // LANE):
        lo = p * LANE
        q2 = q_ref[:, lo:lo + LANE]
        k2 = k_ref[:, lo:lo + LANE]
        v2 = value_pair(v_ref, p)
        outs = []
        for hh in range(2):
            qm = jnp.where(first_half if hh == 0 else ~first_half, q2, jnp.zeros_like(q2))
            outs.append(attend(_dot_nt(qm, k2), v2))
        ona_ref[:, lo:lo + LANE] = jnp.where(first_half, outs[0], outs[1]).astype(BF)
    for p in range(MLA_HEADS // 2):
        lo = p * LANE
        v2 = value_pair(mv_ref, p)
        outs = []
        for hh in range(2):
            hl = (2 * p + hh) * LANE
            outs.append(attend(_dot_nt(mq_ref[:, hl:hl + LANE], mk_ref[:, hl:hl + LANE]), v2))
        omla_ref[:, lo:lo + LANE] = jnp.where(first_half, outs[0], outs[1]).astype(BF)


def _ctx_attn_call(q, k, v, mq, mk, mv, batch):
    nc = q.shape[0]
    c = nc // batch
    ins = [q, k, v, mq, mk, mv]
    ins3 = [a.reshape(batch, c, a.shape[1]) for a in ins]
    spec = lambda w: pl.BlockSpec((None, c, w), lambda b: (b, 0, 0))
    ona, omla = pl.pallas_call(
        _ctx_attn_kernel,
        grid=(batch,),
        in_specs=[spec(a.shape[1]) for a in ins],
        out_specs=[spec(NA_WIDTH), spec(MLA_HEADS * MLA_V)],
        out_shape=[jax.ShapeDtypeStruct((batch, c, NA_WIDTH), BF),
                   jax.ShapeDtypeStruct((batch, c, MLA_HEADS * MLA_V), BF)],
        compiler_params=_params("parallel"),
        name="ctx_attn",
    )(*ins3)
    return ona.reshape(nc, NA_WIDTH), omla.reshape(nc, MLA_HEADS * MLA_V)


LRU_PAD = SUBLANE
LRU_CHUNK = 512
LRU_SEGS = SUBLANE
LRU_SEG_GAP = 4
LRU_TILES = LRU_WIDTH // LANE
LRU_SCAN_UNROLL = 8


def _lru_pitch(n):
    return n // LRU_SEGS + LRU_SEG_GAP


def _lru_pieces(t0, cn, n):
    seg = n // LRU_SEGS
    pitch = _lru_pitch(n)
    out = []
    for k in range(t0 // seg, (t0 + cn - 1) // seg + 1):
        lo, hi = max(t0, k * seg), min(t0 + cn, (k + 1) * seg)
        out.append((lo - t0, hi - t0, k * pitch + lo - k * seg))
    return out


def _lru_coeffs(x_ref, n, xpad_ref, a_ref, u_ref, cw_ref, cb_ref, wg_ref, bg_ref, sp2):
    w = LRU_WIDTH
    zeros = jnp.zeros((LRU_PAD, w), F32)
    xpad_ref[0:LRU_PAD, :] = zeros
    xpad_ref[LRU_PAD:LRU_PAD + n, :] = x_ref[...]
    xpad_ref[LRU_PAD + n:2 * LRU_PAD + n, :] = zeros
    left = (LRU_CONV_W - 1) // 2
    for c0 in range(0, n, LRU_CHUNK):
        cn = min(LRU_CHUNK, n - c0)
        xc = jnp.zeros((cn, w), F32) + cb_ref[...]
        for j in range(LRU_CONV_W):
            off = LRU_PAD + c0 + j - left
            xc = xc + cw_ref[j:j + 1, :] * xpad_ref[off:off + cn, :]
        t = jnp.tanh(_dot(xc.astype(BF), wg_ref[...]) + bg_ref[...])
        xh = 0.5 * xc
        for dr in range(2):
            a = jnp.exp2(sp2[:, dr * w:(dr + 1) * w] * (1.0 + t[:, dr * w:(dr + 1) * w]))
            y = 1.0 - a * a
            root = jnp.where(y > 0.0, y * lax.rsqrt(y), 0.0)
            u = root * ((1.0 + t[:, (2 + dr) * w:(3 + dr) * w]) * xh)
            for lo, hi, row in _lru_pieces(c0, cn, n):
                for j in range(LRU_TILES):
                    a_ref[dr, j, row:row + hi - lo, :] = a[lo:hi, j * LANE:(j + 1) * LANE]
                    u_ref[dr, j, row:row + hi - lo, :] = u[lo:hi, j * LANE:(j + 1) * LANE]


def _lru_scan(n, a_ref, u_ref, hs_ref, h0):
    seg = n // LRU_SEGS
    pitch = _lru_pitch(n)
    chains = [(d, j) for d in range(2) for j in range(LRU_TILES)]

    def rows(d, t):
        return pl.ds(t if d == 0 else seg - 1 - t, LRU_SEGS, stride=pitch)

    def sweep_transfer(t, carry):
        out = []
        for (d, j), (f, p) in zip(chains, carry):
            a = a_ref[d, j, rows(d, t), :]
            out.append((a * f + u_ref[d, j, rows(d, t), :], a * p))
        return tuple(out)

    zeros = jnp.zeros((LRU_SEGS, LANE), F32)
    ones = jnp.ones((LRU_SEGS, LANE), F32)
    transfer = lax.fori_loop(0, seg, sweep_transfer, tuple((zeros, ones) for _ in chains), unroll=LRU_SCAN_UNROLL)

    entries, exits = [], []
    for (d, j), (f, p) in zip(chains, transfer):
        state = h0[d][j]
        entry = [None] * LRU_SEGS
        for k in (range(LRU_SEGS) if d == 0 else range(LRU_SEGS - 1, -1, -1)):
            entry[k] = state
            state = p[k:k + 1, :] * state + f[k:k + 1, :]
        entries.append(jnp.concatenate(entry, axis=0))
        exits.append(state)

    def sweep_states(t, carry):
        out = []
        for (d, j), h in zip(chains, carry):
            h = a_ref[d, j, rows(d, t), :] * h + u_ref[d, j, rows(d, t), :]
            hs_ref[d, j, rows(d, t), :] = h
            out.append(h)
        return tuple(out)

    lax.fori_loop(0, seg, sweep_states, tuple(entries), unroll=LRU_SCAN_UNROLL)
    return [[exits[d * LRU_TILES + j] for j in range(LRU_TILES)] for d in range(2)]


def _lru_output(n, hs_ref, lg_ref, o_ref):
    for c0 in range(0, n, LRU_CHUNK):
        cn = min(LRU_CHUNK, n - c0)
        parts = []
        for lo, hi, row in _lru_pieces(c0, cn, n):
            parts.append(jnp.concatenate(
                [hs_ref[0, j, row:row + hi - lo, :] + hs_ref[1, j, row:row + hi - lo, :] for j in range(LRU_TILES)],
                axis=-1))
        hsum = parts[0] if len(parts) == 1 else jnp.concatenate(parts, axis=0)
        o_ref[c0:c0 + cn, :] = (_gelu_tanh(lg_ref[c0:c0 + cn, :]) * hsum).astype(BF)


def _lru_kernel(lx_ref, lg_ref, clx_ref, clg_ref, cw_ref, cb_ref, wg_ref, bg_ref, lam_ref,
                o_ref, oc_ref, xpad_ref, a_ref, u_ref, hs_ref, ac_ref, uc_ref):
    n = lx_ref.shape[0]
    nc = clx_ref.shape[0]
    nlam = -lam_ref[...]
    sp = jnp.maximum(nlam, 0.0) + jnp.log(1.0 + jnp.exp(-jnp.abs(nlam)))
    sp2 = (-0.5 * LRU_C * LOG2E) * sp
    zero = jnp.zeros((1, LANE), F32)
    _lru_coeffs(clx_ref, nc, xpad_ref, ac_ref, uc_ref, cw_ref, cb_ref, wg_ref, bg_ref, sp2)
    h_ctx = _lru_scan(nc, ac_ref, uc_ref, hs_ref, [[zero] * LRU_TILES] * 2)
    _lru_output(nc, hs_ref, clg_ref, oc_ref)
    _lru_coeffs(lx_ref, n, xpad_ref, a_ref, u_ref, cw_ref, cb_ref, wg_ref, bg_ref, sp2)
    _lru_scan(n, a_ref, u_ref, hs_ref, h_ctx)
    _lru_output(n, hs_ref, lg_ref, o_ref)


def _lru_call(lx, lg, clx, clg, l, wts, batch):
    n, w = lx.shape
    s = n // batch
    c = clx.shape[0] // batch
    seq = lambda a, ln: a.reshape(batch, ln, w)
    lat = pl.BlockSpec((None, s, w), lambda b: (b, 0, 0))
    ctx = pl.BlockSpec((None, c, w), lambda b: (b, 0, 0))
    stacked = [wts["conv_w"], wts["conv_b"], wts["w_gate"], wts["b_gate"], wts["lam"]]
    o, oc = pl.pallas_call(
        _lru_kernel,
        grid=(batch,),
        in_specs=[lat, lat, ctx, ctx] + [_layer_spec(a, l) for a in stacked],
        out_specs=[lat, ctx],
        out_shape=[jax.ShapeDtypeStruct((batch, s, w), BF), jax.ShapeDtypeStruct((batch, c, w), BF)],
        scratch_shapes=[pltpu.VMEM((s + 2 * LRU_PAD, w), F32)]
                       + [pltpu.VMEM((2, LRU_TILES, LRU_SEGS * _lru_pitch(ln), LANE), F32) for ln in (s, s, s, c, c)],
        compiler_params=_params("parallel"),
        name="rglru",
    )(seq(lx, s), seq(lg, s), seq(clx, c), seq(clg, c), *stacked)
    return o.reshape(n, w), oc.reshape(batch * c, w)


def _out_kernel(h_ref, na_ref, mla_ref, lru_ref, gm_ref, shf_ref, scf_ref, gf_ref, g_ref,
                wo_ref, wg_ref, wu_ref, wd_ref, o_ref, *, hid_chunk, sub_rows):
    hidden = wg_ref.shape[1]

    def residual_and_norm(r0):
        rs = slice(r0, r0 + sub_rows)
        o_cat = jnp.concatenate([na_ref[rs, :], mla_ref[rs, :], lru_ref[rs, :]], axis=-1)
        h1 = h_ref[rs, :] + gm_ref[...] * _dot(o_cat, wo_ref[...])
        y = h1 * lax.rsqrt(jnp.mean(h1 * h1, axis=-1, keepdims=True) + EPS) * g_ref[...]
        return h1, (y * (1.0 + scf_ref[...]) + shf_ref[...]).astype(BF)

    def gate_up(y, c0):
        return _dot(y, wg_ref[:, c0:c0 + hid_chunk]), _dot(y, wu_ref[:, c0:c0 + hid_chunk])

    starts = list(range(0, h_ref.shape[0], sub_rows))
    h1, y = residual_and_norm(starts[0])
    nxt = gate_up(y, 0)
    for si, r0 in enumerate(starts):
        acc = None
        h1_next = y_next = None
        for c0 in range(0, hidden, hid_chunk):
            gate, up = nxt
            if c0 + hid_chunk < hidden:
                nxt = gate_up(y, c0 + hid_chunk)
            elif si + 1 < len(starts):
                h1_next, y_next = residual_and_norm(starts[si + 1])
                nxt = gate_up(y_next, 0)
            d = _dot((_silu(gate) * up).astype(BF), wd_ref[c0:c0 + hid_chunk, :])
            acc = d if acc is None else acc + d
        o_ref[r0:r0 + sub_rows, :] = h1 + gf_ref[...] * acc
        h1, y = h1_next, y_next


def _out_call(h, na_o, mla_o, lru_o, mods, row_map, l, wts, tm, sub_rows):
    n, d = h.shape
    row = lambda w: pl.BlockSpec((tm, w), lambda i: (i, 0))
    stacked = [wts["g_ffn"], wts["w_out"], wts["ffn_gate"], wts["ffn_up"], wts["ffn_down"]]
    hidden = wts["ffn_gate"].shape[-1]
    hid_chunk = 2 * LANE if hidden % (2 * LANE) == 0 else hidden
    return pl.pallas_call(
        functools.partial(_out_kernel, hid_chunk=hid_chunk, sub_rows=sub_rows),
        grid=(n // tm,),
        in_specs=[row(d), row(na_o.shape[1]), row(mla_o.shape[1]), row(lru_o.shape[1])]
                 + [_mod_spec(mods, l, which, row_map) for which in (2, 3, 4, 5)]
                 + [_layer_spec(a, l) for a in stacked],
        out_specs=row(d),
        out_shape=jax.ShapeDtypeStruct((n, d), F32),
        compiler_params=_params("parallel"),
        name="out_ffn",
    )(h, na_o, mla_o, lru_o, mods, mods, mods, mods, *stacked)


def _rope_perm():
    j = np.arange(MLA_ROPE)
    first = (j % (MLA_ROPE // 2)) < MLA_ROPE // 4
    partner = np.where(first, j + MLA_ROPE // 4, j - MLA_ROPE // 4)
    sign = np.where(first, -1.0, 1.0).astype(np.float32)
    return partner, sign


def _rope_tables(s):
    t = np.arange(s)
    row = (t // GRID_W).astype(np.float32)
    col = (t % GRID_W).astype(np.float32)
    n_freq = MLA_ROPE // 4
    inv = jnp.asarray(ROPE_BASE, F32) ** (-jnp.arange(n_freq, dtype=F32) / n_freq)
    ar = jnp.asarray(row)[:, None] * inv
    ac = jnp.asarray(col)[:, None] * inv
    ang = jnp.concatenate([ar, ar, ac, ac], axis=-1)
    pad = LANE - MLA_NOPE - MLA_ROPE
    cos = jnp.concatenate([jnp.ones((s, MLA_NOPE), F32), jnp.cos(ang), jnp.zeros((s, pad), F32)], axis=-1)
    sin = jnp.concatenate([jnp.zeros((s, MLA_NOPE), F32), jnp.sin(ang), jnp.zeros((s, pad), F32)], axis=-1)
    return cos, sin


def _lane_tile(nope, rope):
    lead = nope.shape[:-1]
    pad = jnp.zeros(lead + (LANE - MLA_NOPE - MLA_ROPE,), nope.dtype)
    return jnp.concatenate([nope, rope, pad], axis=-1)


def _prep_weights(p):
    partner, sign = _rope_perm()
    w_in = p["w_in"]
    depth, d, _ = w_in.shape
    cols = np.cumsum((NA_WIDTH, NA_WIDTH, NA_WIDTH, MLA_Q_RANK, MLA_KV_RANK, MLA_ROPE, LRU_WIDTH, LRU_WIDTH))
    w_kr = w_in[..., cols[4]:cols[5]]
    zn = jnp.zeros((depth, d, MLA_NOPE), F32)
    w_all = jnp.concatenate([
        w_in[..., :cols[4]], _lane_tile(zn, w_kr), _lane_tile(zn, w_kr[..., partner]), w_in[..., cols[5]:]], axis=-1)
    assert w_all.shape[-1] == W_ALL

    wq = p["mla_w_qb"].reshape(depth, MLA_Q_RANK, MLA_HEADS, MLA_NOPE + MLA_ROPE)
    wq_n, wq_r = wq[..., :MLA_NOPE], wq[..., MLA_NOPE:]
    w_qb = jnp.concatenate([
        _lane_tile(wq_n, wq_r).reshape(depth, MLA_Q_RANK, MLA_QK_W),
        _lane_tile(jnp.zeros_like(wq_n), wq_r[..., partner]).reshape(depth, MLA_Q_RANK, MLA_QK_W)], axis=-1)

    wkv = p["mla_w_kvb"].reshape(depth, MLA_KV_RANK, MLA_HEADS, MLA_NOPE + MLA_V)
    wk_n, wv = wkv[..., :MLA_NOPE], wkv[..., MLA_NOPE:]
    w_kvb = jnp.concatenate([
        _lane_tile(wk_n, jnp.zeros(wk_n.shape[:-1] + (MLA_ROPE,), F32)).reshape(depth, MLA_KV_RANK, MLA_QK_W),
        wv.reshape(depth, MLA_KV_RANK, MLA_HEADS * MLA_V)], axis=-1)

    qg, kg = p["mla_q_g"], p["mla_k_g"]
    z_n, z_r = jnp.zeros((depth, MLA_NOPE), F32), jnp.zeros((depth, MLA_ROPE), F32)
    sgn = jnp.asarray(sign)
    zrow = jnp.zeros((depth, LANE), F32)
    g_mla = jnp.stack([
        _lane_tile(qg[:, :MLA_NOPE], qg[:, MLA_NOPE:]) * MLA_SCALE,
        _lane_tile(z_n, sgn * qg[:, MLA_NOPE:][:, partner]) * MLA_SCALE,
        _lane_tile(kg[:, :MLA_NOPE], z_r),
        _lane_tile(z_n, kg[:, MLA_NOPE:]),
        _lane_tile(z_n, sgn * kg[:, MLA_NOPE:][:, partner]),
        zrow, zrow, zrow], axis=1)

    g_na = jnp.concatenate([jnp.tile(p["na_q_g"], (1, NA_HEADS)) * NA_SCALE, jnp.tile(p["na_k_g"], (1, NA_HEADS))],
                           axis=-1)[:, None]

    wa, wx = p["lru_w_a"], p["lru_w_x"]
    wg4 = jnp.stack([wa[:, 0], wa[:, 1], wx[:, 0], wx[:, 1]], axis=1)
    eye = jnp.eye(LRU_HEADS, dtype=F32)
    w_gate = (wg4[:, :, :, :, None, :] * eye[None, None, :, None, :, None])
    w_gate = w_gate.transpose(0, 2, 3, 1, 4, 5).reshape(depth, LRU_WIDTH, 4 * LRU_WIDTH)
    ba, bx = p["lru_b_a"], p["lru_b_x"]
    b_gate = jnp.concatenate([ba[:, 0], ba[:, 1], bx[:, 0], bx[:, 1]], axis=-1)[:, None]

    return {
        "g_mix": p["norm_mix_g"][:, None], "g_ffn": p["norm_ffn_g"][:, None],
        "w_all": w_all.astype(BF), "w_qb": w_qb.astype(BF), "w_kvb": w_kvb.astype(BF),
        "g_na": g_na, "g_cq": p["mla_cq_g"][:, None], "g_ckv": p["mla_ckv_g"][:, None], "g_mla": g_mla,
        "conv_w": p["lru_conv_w"], "conv_b": p["lru_conv_b"][:, None],
        "w_gate": (0.5 * w_gate).astype(BF), "b_gate": 0.5 * b_gate, "lam": p["lru_lambda"].reshape(depth, 1, 2 * LRU_WIDTH),
        "w_out": p["w_out"].astype(BF), "ffn_gate": p["ffn_w_gate"].astype(BF),
        "ffn_up": p["ffn_w_up"].astype(BF), "ffn_down": p["ffn_w_down"].astype(BF),
    }


def kernel(x, c, ctx, c_ctx, ada_w, ada_b, norm_mix_g, norm_ffn_g, w_in, na_q_g, na_k_g, na_rpb, mla_cq_g, mla_w_qb, mla_ckv_g, mla_w_kvb, mla_q_g, mla_k_g, lru_conv_w, lru_conv_b, lru_w_a, lru_b_a, lru_w_x, lru_b_x, lru_lambda, w_out, ffn_w_gate, ffn_w_up, ffn_w_down):
    batch, s, d = x.shape
    cl = ctx.shape[1]
    depth = ada_w.shape[0]
    rows = s // GRID_W
    assert s % GRID_W == 0 and rows % (NA_BLOCKS_PER_STEP * NA_QROWS) == 0 and rows >= 2 * NA_KROWS
    p = dict(norm_mix_g=norm_mix_g, norm_ffn_g=norm_ffn_g, w_in=w_in, na_q_g=na_q_g, na_k_g=na_k_g,
             mla_cq_g=mla_cq_g, mla_w_qb=mla_w_qb, mla_ckv_g=mla_ckv_g, mla_w_kvb=mla_w_kvb, mla_q_g=mla_q_g,
             mla_k_g=mla_k_g, lru_conv_w=lru_conv_w, lru_conv_b=lru_conv_b, lru_w_a=lru_w_a, lru_b_a=lru_b_a,
             lru_w_x=lru_w_x, lru_b_x=lru_b_x, lru_lambda=lru_lambda, w_out=w_out, ffn_w_gate=ffn_w_gate,
             ffn_w_up=ffn_w_up, ffn_w_down=ffn_w_down)

    sub = 512 if s % 512 == 0 else GRID_W * NA_QROWS
    tm = 2 * sub if s % (2 * sub) == 0 else sub
    subc = sub if (batch * cl) % sub == 0 else cl
    tmc = 2 * subc if (batch * cl) % (2 * subc) == 0 else subc
    tq = 512

    nmod = -(-(batch + 1) // SUBLANE) * SUBLANE
    cvec = jnp.concatenate([c, c_ctx[None], jnp.zeros((nmod - batch - 1, d), F32)], axis=0)
    mods = _ada_call(cvec, ada_w, ada_b).reshape(depth, nmod, 6, 1, d)
    tiles_per_batch = s // tm
    lat_row = lambda i: i // tiles_per_batch
    ctx_row = lambda i: batch

    wts = _prep_weights(p)
    na_bias = _na_bias_pieces(na_rpb, rows)
    cos, sin = _rope_tables(s)
    cos_c = jnp.concatenate([jnp.ones((tmc, MLA_NOPE + MLA_ROPE), F32),
                             jnp.zeros((tmc, LANE - MLA_NOPE - MLA_ROPE), F32)], axis=-1)
    sin_c = jnp.zeros((tmc, LANE), F32)

    h = x.reshape(batch * s, d)
    hc = ctx.reshape(batch * cl, d)
    for l in range(depth):
        last = l == depth - 1
        q, k, v, mq, mk, mv, lx, lg = _in_call(h, mods, lat_row, l, wts, cos, sin, tm, sub // 2)
        cq_, ck, cv, cmq, cmk, cmv, clx, clg = _in_call(hc, mods, ctx_row, l, wts, cos_c, sin_c, tmc, subc // 2)

        na_o = _na_call(q, k, v, ck, cv, na_bias, l, batch)
        mla_o = _mla_call(mq, mk, mv, cmk, cmv, batch, tq)
        lru_o, lru_c = _lru_call(lx, lg, clx, clg, l, wts, batch)
        h = _out_call(h, na_o, mla_o, lru_o, mods, lat_row, l, wts, tm, sub)

        if not last:
            na_c, mla_c = _ctx_attn_call(cq_, ck, cv, cmq, cmk, cmv, batch)
            hc = _out_call(hc, na_c, mla_c, lru_c, mods, ctx_row, l, wts, tmc, subc)
    return h.reshape(batch, s, d)
```
